```python
import math
import jax, jax.numpy as jnp
from jax import lax
import numpy as np

D_MODEL = 4096
BATCH = 1
SEQ = 8192
DEPTH = 1
DEC_BATCH = 32
DEC_SEQ = 4
PAST_LEN = 8192
PAGE_SIZE = 128

W_A = D_MODEL // 2
DK_A = 128
DV_A = 2 * DK_A
H_A = W_A // DV_A
ROPE_THETA = 500000.0
ROT_DIM = DK_A // 4
Q_BLOCK = 128
W_B = D_MODEL // 2
H_B = 4
DV_B = W_B // H_B
DK_B = DV_B // 2
GLA_RANK = 16
GLA_TAU = 16.0
GLA_CHUNK = 32
N_EXPERTS = 32
TOP_K = 4
D_EXPERT = D_MODEL
SWIGLU_LIMIT = 7.0
SWIGLU_ALPHA = 1.702
MOE_BLOCK = 128
NORM_EPS = 1e-5
IN_SIZES = (2 * H_A * DK_A, 2 * H_A * DK_A, H_A * DV_A, H_B * DK_B, H_B * DK_B, H_B * DV_B, H_B * DV_B, D_MODEL, D_MODEL, GLA_RANK)
N_IN = sum(IN_SIZES)

kernel_name = 'hybrid_diffattn_gla_moe_step'


def lambda_init(layer):
    return 0.8 - 0.6 * math.exp(-0.3 * layer)


def rmsnorm(x, w):
    xf = x.astype(jnp.float32)
    y = xf * lax.rsqrt(jnp.mean(xf * xf, axis=-1, keepdims=True) + NORM_EPS)
    return (y * w.astype(jnp.float32)).astype(x.dtype)


def rope_partial(x, pos):
    half = ROT_DIM // 2
    inv_freq = jnp.power(jnp.float32(ROPE_THETA), -jnp.arange(half, dtype=jnp.float32) * (2.0 / ROT_DIM))
    ang = pos.astype(jnp.float32)[:, None] * inv_freq[None, :]
    cos = jnp.cos(ang)[:, None, :].astype(x.dtype)
    sin = jnp.sin(ang)[:, None, :].astype(x.dtype)
    x1 = x[..., :half]
    x2 = x[..., half:ROT_DIM]
    return jnp.concatenate([x1 * cos - x2 * sin, x2 * cos + x1 * sin, x[..., ROT_DIM:]], axis=-1)


def diff_core(q, k, v, q_pos, k_pos, lam):
    s = jnp.einsum('lmd,tmd->mlt', q, k).astype(jnp.float32) * (DK_A ** -0.5)
    s = jnp.where(k_pos[None, None, :] <= q_pos[None, :, None], s, -jnp.inf)
    p = jax.nn.softmax(s, axis=-1).reshape(H_A, 2, q.shape[0], k.shape[0])
    a = p[:, 0] - lam * p[:, 1]
    return jnp.einsum('hlt,thv->lhv', a.astype(v.dtype), v)


def attend_prompt(l, q, k, v, lam):
    B, S = q.shape[0], q.shape[1]
    qb_size = math.gcd(S, Q_BLOCK)
    nb = S // qb_size
    pos = jnp.arange(S, dtype=jnp.int32)
    q_blocks = q.reshape(B, nb, qb_size, 2 * H_A, DK_A).swapaxes(0, 1)
    qpos_blocks = pos.reshape(nb, qb_size)
    core_b = jax.vmap(diff_core, in_axes=(0, 0, 0, None, None, None))

    def one_block(args):
        qblk, qp = args
        return core_b(qblk, k, v, qp, pos, lam)

    o = lax.map(one_block, (q_blocks, qpos_blocks))
    return o.swapaxes(0, 1).reshape(B, S, H_A, DV_A)


def make_sample_attend(cache_k, cache_v, page_table):
    def attend(l, q, k, v, lam):
        past = page_table.shape[1] * PAGE_SIZE
        L = q.shape[1]
        q_pos = past + jnp.arange(L, dtype=jnp.int32)
        k_pos = jnp.arange(past + L, dtype=jnp.int32)

        def one_seq(args):
            pt, qs, ks, vs = args
            kp = cache_k[l, pt].reshape(past, 2 * H_A, DK_A).astype(ks.dtype)
            vp = cache_v[l, pt].reshape(past, H_A, DV_A).astype(vs.dtype)
            return diff_core(qs, jnp.concatenate([kp, ks], axis=0), jnp.concatenate([vp, vs], axis=0), q_pos, k_pos, lam)

        return lax.map(one_seq, (page_table, q, k, v))
    return attend


def gla_chunked(q, k, v, log_a, S0):
    B, L, H, dk = q.shape
    dv = v.shape[-1]
    C = math.gcd(L, GLA_CHUNK)
    n = L // C

    def to_chunks(t):
        return t.reshape((B, n, C) + t.shape[2:]).swapaxes(0, 1)

    causal = jnp.tril(jnp.ones((C, C), dtype=bool))

    def step(S, inp):
        qc, kc, vc, gc = inp
        qf = qc.astype(jnp.float32)
        kf = kc.astype(jnp.float32)
        vf = vc.astype(jnp.float32)
        b = jnp.cumsum(gc.astype(jnp.float32), axis=1)
        o_inter = jnp.einsum('bihk,bhkv->bihv', qf * jnp.exp(b), S)
        diff = b[:, :, None] - b[:, None, :]
        decay = jnp.exp(jnp.where(causal[None, :, :, None, None], diff, -jnp.inf))
        A = jnp.einsum('bihk,bjhk,bijhk->bhij', qf, kf, decay)
        o_intra = jnp.einsum('bhij,bjhv->bihv', A, vf)
        b_last = b[:, -1]
        S_new = jnp.exp(b_last)[..., None] * S + jnp.einsum('bjhk,bjhv->bhkv', kf * jnp.exp(b_last[:, None] - b), vf)
        return S_new, (o_inter + o_intra).astype(v.dtype)

    S_fin, o = lax.scan(step, S0, (to_chunks(q), to_chunks(k), to_chunks(v), to_chunks(log_a)))
    o = o.swapaxes(0, 1).reshape(B, L, H, dv)
    return o, S_fin


def token_mix(u, pos, l, attend, S0, w_in, lambda_q1, lambda_k1, lambda_q2, lambda_k2, subln_w, w_alpha2, b_alpha, gla_norm_w, w_branch_a, w_branch_b, w_out):
    B, L, _ = u.shape
    splits = [int(s) for s in np.cumsum(IN_SIZES)[:-1]]
    z = u @ w_in[l]
    q_a, k_a, v_a, q_b, k_b, v_b, r_b, g_a, g_b, a_lr = jnp.split(z, splits, axis=-1)
    q_a = rope_partial(q_a.reshape(B, L, 2 * H_A, DK_A), pos)
    k_a = rope_partial(k_a.reshape(B, L, 2 * H_A, DK_A), pos)
    v_a = v_a.reshape(B, L, H_A, DV_A)
    lam_init = lambda_init(l)
    lam = (jnp.exp(jnp.sum(lambda_q1[l].astype(jnp.float32) * lambda_k1[l].astype(jnp.float32)))
           - jnp.exp(jnp.sum(lambda_q2[l].astype(jnp.float32) * lambda_k2[l].astype(jnp.float32))) + lam_init)
    o_a = attend(l, q_a, k_a, v_a, lam)
    o_a = rmsnorm(o_a, subln_w[l]) * (1.0 - lam_init)
    y_a = o_a.reshape(B, L, W_A) @ w_branch_a[l]
    qg = (q_b * (DK_B ** -0.5)).reshape(B, L, H_B, DK_B)
    kg = k_b.reshape(B, L, H_B, DK_B)
    vg = v_b.reshape(B, L, H_B, DV_B)
    log_a = (jax.nn.log_sigmoid((a_lr @ w_alpha2[l] + b_alpha[l]).astype(jnp.float32)) / GLA_TAU).reshape(B, L, H_B, DK_B)
    o_b, S_new = gla_chunked(qg, kg, vg, log_a, S0)
    o_b = rmsnorm(o_b, gla_norm_w[l]) * jax.nn.silu(r_b.reshape(B, L, H_B, DV_B))
    y_b = o_b.reshape(B, L, W_B) @ w_branch_b[l]
    mix = jax.nn.sigmoid(g_a) * y_a + jax.nn.sigmoid(g_b) * y_b
    return mix @ w_out[l], k_a, v_a, S_new


def moe(xf, l, w_router, b_router, w_gate_up, b_gate_up, w_down, b_down):
    N, D = xf.shape
    logits = (xf @ w_router[l]).astype(jnp.float32) + b_router[l].astype(jnp.float32)
    top_v, top_i = lax.top_k(logits, TOP_K)
    gates = jax.nn.softmax(top_v, axis=-1).astype(xf.dtype)
    M = N * TOP_K
    per_expert = -(-M // N_EXPERTS)
    blk = MOE_BLOCK
    while blk > 8 and blk // 2 >= per_expert:
        blk //= 2
    e_flat = top_i.reshape(-1)
    tok = jnp.arange(M, dtype=jnp.int32) // TOP_K
    counts = jnp.bincount(e_flat, length=N_EXPERTS)
    padded = (counts + blk - 1) // blk * blk
    pad_end = jnp.cumsum(padded)
    pad_start = pad_end - padded
    cnt_start = jnp.cumsum(counts) - counts
    order = jnp.argsort(e_flat, stable=True)
    se = e_flat[order]
    dest = pad_start[se] + jnp.arange(M, dtype=jnp.int32) - cnt_start[se]
    n_blocks = -(-M // blk) + N_EXPERTS
    n_slots = n_blocks * blk
    slot_tok = jnp.full((n_slots,), N, dtype=jnp.int32).at[dest].set(tok[order])
    slot_gate = jnp.zeros((n_slots,), xf.dtype).at[dest].set(gates.reshape(-1)[order])
    block_e = jnp.minimum(jnp.searchsorted(pad_end, jnp.arange(n_blocks, dtype=pad_end.dtype) * blk, side='right'), N_EXPERTS - 1)
    x_pad = jnp.concatenate([xf, jnp.zeros((1, D), xf.dtype)], axis=0)

    def expert_block(args):
        idx, g, e = args
        xb = x_pad[idx]
        gu = xb @ w_gate_up[l, e] + b_gate_up[l, e]
        gt = jnp.minimum(gu[:, :D_EXPERT], SWIGLU_LIMIT)
        up = jnp.clip(gu[:, D_EXPERT:], -SWIGLU_LIMIT, SWIGLU_LIMIT)
        act = (up + 1.0) * (gt * jax.nn.sigmoid(SWIGLU_ALPHA * gt))
        return (act @ w_down[l, e] + b_down[l, e]) * g[:, None]

    yb = lax.map(expert_block, (slot_tok.reshape(n_blocks, blk), slot_gate.reshape(n_blocks, blk), block_e))
    y = jnp.zeros((N + 1, D), xf.dtype).at[slot_tok].add(yb.reshape(n_slots, D))
    return y[:N]


def run_group(x, pos, attend, gla_init, norm_mix_w, w_in, lambda_q1, lambda_k1, lambda_q2, lambda_k2, subln_w, w_alpha2, b_alpha, gla_norm_w, w_branch_a, w_branch_b, w_out, norm_ffn_w, w_router, b_router, w_gate_up, b_gate_up, w_down, b_down, norm_final_w):
    k_rows, v_rows, states = [], [], []
    for l in range(DEPTH):
        u = rmsnorm(x, norm_mix_w[l])
        m, k_new, v_new, s_new = token_mix(u, pos, l, attend, gla_init(l), w_in, lambda_q1, lambda_k1, lambda_q2, lambda_k2, subln_w, w_alpha2, b_alpha, gla_norm_w, w_branch_a, w_branch_b, w_out)
        x = x + m
        h = rmsnorm(x, norm_ffn_w[l])
        x = x + moe(h.reshape(-1, D_MODEL), l, w_router, b_router, w_gate_up, b_gate_up, w_down, b_down).reshape(x.shape)
        k_rows.append(k_new)
        v_rows.append(v_new)
        states.append(s_new)
    y = rmsnorm(x, norm_final_w)
    return y, jnp.stack(k_rows, 0), jnp.stack(v_rows, 0), jnp.stack(states, 0)


def setup_inputs(seed: int = 0) -> dict:
    key = jax.random.key(seed)
    ks = jax.random.split(key, 32)
    f32 = jnp.float32
    n_pages = PAST_LEN // PAGE_SIZE
    n_pool = (DEC_BATCH * n_pages * 5) // 4

    def nrm(k, shape, scale):
        return jax.random.normal(k, shape, f32) * scale

    def gain(k, shape):
        return 1.0 + 0.01 * jax.random.normal(k, shape, f32)

    page_table = jax.random.permutation(ks[5], n_pool)[:DEC_BATCH * n_pages].reshape(DEC_BATCH, n_pages).astype(jnp.int32)
    return {
        'x_prompt': nrm(ks[0], (BATCH, SEQ, D_MODEL), 1.0),
        'x_sample': nrm(ks[1], (DEC_BATCH, DEC_SEQ, D_MODEL), 1.0),
        'cache_k': nrm(ks[2], (DEPTH, n_pool, PAGE_SIZE, 2 * H_A, DK_A), 1.0),
        'cache_v': nrm(ks[3], (DEPTH, n_pool, PAGE_SIZE, H_A, DV_A), 1.0),
        'state_gla': nrm(ks[4], (DEPTH, DEC_BATCH, H_B, DK_B, DV_B), 3.0),
        'page_table': page_table,
        'norm_mix_w': gain(ks[6], (DEPTH, D_MODEL)),
        'w_in': nrm(ks[7], (DEPTH, D_MODEL, N_IN), D_MODEL ** -0.5),
        'lambda_q1': nrm(ks[8], (DEPTH, DK_A), 0.1),
        'lambda_k1': nrm(ks[9], (DEPTH, DK_A), 0.1),
        'lambda_q2': nrm(ks[10], (DEPTH, DK_A), 0.1),
        'lambda_k2': nrm(ks[11], (DEPTH, DK_A), 0.1),
        'subln_w': gain(ks[12], (DEPTH, DV_A)),
        'w_alpha2': nrm(ks[13], (DEPTH, GLA_RANK, H_B * DK_B), GLA_RANK ** -0.5),
        'b_alpha': nrm(ks[14], (DEPTH, H_B * DK_B), 0.1),
        'gla_norm_w': gain(ks[15], (DEPTH, DV_B)),
        'w_branch_a': nrm(ks[16], (DEPTH, W_A, D_MODEL), W_A ** -0.5),
        'w_branch_b': nrm(ks[17], (DEPTH, W_B, D_MODEL), W_B ** -0.5),
        'w_out': nrm(ks[18], (DEPTH, D_MODEL, D_MODEL), D_MODEL ** -0.5),
        'norm_ffn_w': gain(ks[19], (DEPTH, D_MODEL)),
        'w_router': nrm(ks[20], (DEPTH, D_MODEL, N_EXPERTS), D_MODEL ** -0.5),
        'b_router': nrm(ks[21], (DEPTH, N_EXPERTS), 0.01),
        'w_gate_up': nrm(ks[22], (DEPTH, N_EXPERTS, D_MODEL, 2 * D_EXPERT), D_MODEL ** -0.5),
        'b_gate_up': nrm(ks[23], (DEPTH, N_EXPERTS, 2 * D_EXPERT), 0.01),
        'w_down': nrm(ks[24], (DEPTH, N_EXPERTS, D_EXPERT, D_MODEL), D_EXPERT ** -0.5),
        'b_down': nrm(ks[25], (DEPTH, N_EXPERTS, D_MODEL), 0.01),
        'norm_final_w': gain(ks[26], (D_MODEL,)),
    }


def reference(x_prompt, x_sample, cache_k, cache_v, state_gla, page_table, norm_mix_w, w_in, lambda_q1, lambda_k1, lambda_q2, lambda_k2, subln_w, w_alpha2, b_alpha, gla_norm_w, w_branch_a, w_branch_b, w_out, norm_ffn_w, w_router, b_router, w_gate_up, b_gate_up, w_down, b_down, norm_final_w):
    weights = (norm_mix_w, w_in, lambda_q1, lambda_k1, lambda_q2, lambda_k2, subln_w, w_alpha2, b_alpha, gla_norm_w, w_branch_a, w_branch_b, w_out, norm_ffn_w, w_router, b_router, w_gate_up, b_gate_up, w_down, b_down, norm_final_w)
    B = x_prompt.shape[0]
    pos_p = jnp.arange(x_prompt.shape[1], dtype=jnp.int32)
    past = page_table.shape[1] * PAGE_SIZE
    pos_s = past + jnp.arange(x_sample.shape[1], dtype=jnp.int32)

    def zero_state(l):
        return jnp.zeros((B, H_B, DK_B, DV_B), jnp.float32)

    def carried_state(l):
        return state_gla[l].astype(jnp.float32)

    y_prompt, k_prompt, v_prompt, gla_prompt = run_group(x_prompt, pos_p, attend_prompt, zero_state, *weights)
    y_sample, k_sample, v_sample, gla_sample = run_group(x_sample, pos_s, make_sample_attend(cache_k, cache_v, page_table), carried_state, *weights)
    return (y_prompt, y_sample, k_prompt, v_prompt, gla_prompt, k_sample, v_sample, gla_sample)
```

```python
import functools
import math

import jax
import jax.numpy as jnp
import numpy as np
from jax import lax
from jax.experimental import pallas as pl
from jax.experimental.pallas import tpu as pltpu

F32 = jnp.float32
BF16 = jnp.bfloat16

ROPE_THETA = 500000.0
GLA_TAU = 16.0
TOP_K = 4
SWIGLU_LIMIT = 7.0
SWIGLU_ALPHA = 1.702
NORM_EPS = 1e-5
LAMBDA_INIT = 0.8 - 0.6 * math.exp(-0.3 * 0)

LANES = 128
SUBLANES = 8
BF16_SUBLANES = 16
VMEM_LIMIT_BYTES = 56 * 1024 * 1024

ROW_TILE = 256
MM_TM = 1024
MM_TN = 512
ATTN_TILE = 512
PAGES_PER_STEP = 4
GLA_CHUNK = 128
MOE_TM = 128
MOE_TN = 256
MOE_GROUP_SLACK = 1.2
GATHER_ROWS = 128
COMBINE_ROWS = 32


def _tile(n, pref, mult):
    best = None
    for d in range(mult, min(n, pref) + 1, mult):
        if n % d == 0:
            best = d
    return best if best is not None else n


def _params(sem, vmem=None):
    return pltpu.CompilerParams(dimension_semantics=sem, vmem_limit_bytes=vmem)


def _sigmoid(x):
    return 1.0 / (1.0 + jnp.exp(-x))


def _nt_dot(a, b):
    return lax.dot_general(a, b, (((1,), (1,)), ((), ())), preferred_element_type=F32)


def _tn_dot(a, b):
    return lax.dot_general(a, b, (((0,), (0,)), ((), ())), preferred_element_type=F32)


def _rmsnorm_body(x_ref, w_ref, o_ref):
    x = x_ref[...].astype(F32)
    y = x * lax.rsqrt(jnp.mean(x * x, axis=-1, keepdims=True) + NORM_EPS)
    o_ref[...] = (y * w_ref[...]).astype(o_ref.dtype)


def _rmsnorm(x, w, out_dtype):
    n, d = x.shape
    tr = _tile(n, ROW_TILE, BF16_SUBLANES)
    return pl.pallas_call(
        _rmsnorm_body,
        grid=(n // tr,),
        in_specs=[pl.BlockSpec((tr, d), lambda i: (i, 0)), pl.BlockSpec((1, d), lambda i: (0, 0))],
        out_specs=pl.BlockSpec((tr, d), lambda i: (i, 0)),
        out_shape=jax.ShapeDtypeStruct((n, d), out_dtype),
        compiler_params=_params(("parallel",)),
        name="rmsnorm",
    )(x, w.reshape(1, d).astype(F32))


def _mm_body(*refs, n_pairs, epilogue):
    o_ref = refs[-1]
    accs = [
        jnp.dot(refs[2 * p][...].astype(BF16), refs[2 * p + 1][...].astype(BF16), preferred_element_type=F32)
        for p in range(n_pairs)
    ]
    extras = [r[...] for r in refs[2 * n_pairs:-1]]
    o_ref[...] = epilogue(accs, extras).astype(o_ref.dtype)


def _matmul(pairs, extras, n_out, epilogue, out_dtype, name):
    m = pairs[0][0].shape[0]
    tm = _tile(m, MM_TM, BF16_SUBLANES)
    tn = _tile(n_out, MM_TN, LANES)
    in_specs, args = [], []
    for a, b, off in pairs:
        k = a.shape[1]
        assert off % tn == 0 and b.shape[0] == k
        in_specs.append(pl.BlockSpec((tm, k), lambda i, j: (i, 0)))
        in_specs.append(pl.BlockSpec((k, tn), functools.partial(lambda i, j, o: (0, j + o), o=off // tn)))
        args += [a, b]
    for e, off in extras:
        assert off % tn == 0
        in_specs.append(pl.BlockSpec((tm, tn), functools.partial(lambda i, j, o: (i, j + o), o=off // tn)))
        args.append(e)
    return pl.pallas_call(
        functools.partial(_mm_body, n_pairs=len(pairs), epilogue=epilogue),
        grid=(m // tm, n_out // tn),
        in_specs=in_specs,
        out_specs=pl.BlockSpec((tm, tn), lambda i, j: (i, j)),
        out_shape=jax.ShapeDtypeStruct((m, n_out), out_dtype),
        compiler_params=_params(("parallel", "arbitrary"), VMEM_LIMIT_BYTES),
        name=name,
    )(*args)


def _epi_plain(accs, extras):
    return accs[0]


def _epi_gated_merge(accs, extras):
    return _sigmoid(extras[0]) * accs[0] + _sigmoid(extras[1]) * accs[1]


def _epi_residual(accs, extras):
    return extras[0] + accs[0]


def _decay_body(u_ref, wlr_ref, wa2_ref, ba_ref, g_ref):
    a = jnp.dot(u_ref[...], wlr_ref[...], preferred_element_type=F32)
    x = jnp.dot(a.astype(BF16), wa2_ref[...], preferred_element_type=F32) + ba_ref[...]
    g_ref[...] = (jnp.minimum(x, 0.0) - jnp.log1p(jnp.exp(-jnp.abs(x)))) * (1.0 / GLA_TAU)


def _decay(u, w_lr, w_alpha2, b_alpha):
    n, d = u.shape
    rank, c = w_alpha2.shape
    assert rank <= LANES
    wlr = jnp.zeros((d, LANES), BF16).at[:, :rank].set(w_lr.astype(BF16))
    wa2 = jnp.zeros((LANES, c), BF16).at[:rank].set(w_alpha2.astype(BF16))
    tr = _tile(n, 2 * ROW_TILE, BF16_SUBLANES)
    return pl.pallas_call(
        _decay_body,
        grid=(n // tr,),
        in_specs=[
            pl.BlockSpec((tr, d), lambda i: (i, 0)),
            pl.BlockSpec((d, LANES), lambda i: (0, 0)),
            pl.BlockSpec((LANES, c), lambda i: (0, 0)),
            pl.BlockSpec((1, c), lambda i: (0, 0)),
        ],
        out_specs=pl.BlockSpec((tr, c), lambda i: (i, 0)),
        out_shape=jax.ShapeDtypeStruct((n, c), F32),
        compiler_params=_params(("parallel",)),
        name="gla_decay",
    )(u, wlr, wa2, b_alpha.reshape(1, c).astype(F32))


def _rope_tables(pos, dk):
    rot = dk // 4
    half = rot // 2
    inv_freq = jnp.power(jnp.float32(ROPE_THETA), -jnp.arange(half, dtype=F32) * (2.0 / rot))
    ang = pos.astype(F32)[:, None] * inv_freq[None, :]
    cos, sin = jnp.cos(ang), jnp.sin(ang)
    n = pos.shape[0]
    pad = jnp.zeros((n, dk - rot), F32)
    zero = jnp.zeros((n, half), F32)
    c = jnp.concatenate([cos, cos, pad + 1.0], axis=1)
    s1 = jnp.concatenate([-sin, zero, pad], axis=1)
    s2 = jnp.concatenate([zero, sin, pad], axis=1)
    return c, s1, s2


def _prep_body(zq_ref, zk_ref, zv_ref, c_ref, s1_ref, s2_ref, q_o, kf_o, kb_o, vf_o, vb_o, *, n_maps, dk):
    half = dk // 8
    c, s1, s2 = c_ref[...], s1_ref[...], s2_ref[...]

    def rope(x):
        return x * c + pltpu.roll(x, dk - half, 1) * s1 + pltpu.roll(x, half, 1) * s2

    for m in range(n_maps):
        sl = slice(m * dk, (m + 1) * dk)
        q_o[:, sl] = rope(zq_ref[:, sl]).astype(q_o.dtype)
        k = rope(zk_ref[:, sl])
        kf_o[:, sl] = k
        kb_o[:, sl] = k.astype(kb_o.dtype)
    v = zv_ref[...]
    vf_o[...] = v
    vb_o[...] = v.astype(vb_o.dtype)


def _prep(z, row0, n_rows, pos, n_maps, dk, w_v, offs):
    w_qk = n_maps * dk
    tr = _tile(n_rows, ROW_TILE, BF16_SUBLANES)
    assert row0 % tr == 0 and offs[0] % w_qk == 0 and offs[1] % w_qk == 0 and offs[2] % w_v == 0
    rb = row0 // tr
    c, s1, s2 = _rope_tables(pos, dk)
    zspec = lambda w, off: pl.BlockSpec((tr, w), functools.partial(lambda i, o: (i + rb, o), o=off // w))
    tspec = pl.BlockSpec((tr, dk), lambda i: (i, 0))
    ospec = lambda w: pl.BlockSpec((tr, w), lambda i: (i, 0))
    sds = lambda w, dt: jax.ShapeDtypeStruct((n_rows, w), dt)
    return pl.pallas_call(
        functools.partial(_prep_body, n_maps=n_maps, dk=dk),
        grid=(n_rows // tr,),
        in_specs=[zspec(w_qk, offs[0]), zspec(w_qk, offs[1]), zspec(w_v, offs[2]), tspec, tspec, tspec],
        out_specs=[ospec(w_qk), ospec(w_qk), ospec(w_qk), ospec(w_v), ospec(w_v)],
        out_shape=[sds(w_qk, BF16), sds(w_qk, F32), sds(w_qk, BF16), sds(w_v, F32), sds(w_v, BF16)],
        compiler_params=_params(("parallel",), VMEM_LIMIT_BYTES),
        name="qkv_rope",
    )(z, z, z, c, s1, s2)


def _lambda_value(lq1, lk1, lq2, lk2):
    a = jnp.exp(jnp.sum(lq1[...] * lk1[...], axis=-1, keepdims=True))
    b = jnp.exp(jnp.sum(lq2[...] * lk2[...], axis=-1, keepdims=True))
    return a - b + LAMBDA_INIT


def _diff_finish(a1, l1, a2, l2, lam, sw):
    o = a1 / l1 - lam * (a2 / l2)
    y = o * lax.rsqrt(jnp.mean(o * o, axis=-1, keepdims=True) + NORM_EPS)
    return (y * sw) * (1.0 - LAMBDA_INIT)


def _online_softmax_update(s, scale, m_prev, l_prev):
    m_new = jnp.maximum(m_prev, jnp.max(s, axis=-1, keepdims=True))
    alpha = jnp.exp((m_prev - m_new) * scale)
    p = jnp.exp((s - m_new) * scale)
    return m_new, alpha, p, alpha * l_prev + jnp.sum(p, axis=-1, keepdims=True)


def _attn_prompt_body(qi_ref, kj_ref, q_ref, k_ref, v_ref, lq1, lk1, lq2, lk2, sw_ref, o_ref,
                      m_sc, l_sc, acc_sc, *, dk, scale):
    t = pl.program_id(1)
    i = qi_ref[t]
    j = kj_ref[t]

    @pl.when(j == 0)
    def _():
        m_sc[...] = jnp.full(m_sc.shape, -jnp.inf, F32)
        l_sc[...] = jnp.zeros(l_sc.shape, F32)
        acc_sc[...] = jnp.zeros(acc_sc.shape, F32)

    def step(masked):
        v = v_ref[...]
        for c in range(2):
            s = _nt_dot(q_ref[:, c * dk:(c + 1) * dk], k_ref[:, c * dk:(c + 1) * dk])
            if masked:
                row = lax.broadcasted_iota(jnp.int32, s.shape, 0)
                col = lax.broadcasted_iota(jnp.int32, s.shape, 1)
                s = jnp.where(col <= row, s, -jnp.inf)
            m_new, alpha, p, l_new = _online_softmax_update(s, scale, m_sc[c], l_sc[c])
            acc_sc[c] = alpha * acc_sc[c] + jnp.dot(p.astype(BF16), v, preferred_element_type=F32)
            m_sc[c] = m_new
            l_sc[c] = l_new

    @pl.when(j < i)
    def _():
        step(False)

    @pl.when(j == i)
    def _():
        step(True)
        lam = _lambda_value(lq1, lk1, lq2, lk2)
        o_ref[...] = _diff_finish(acc_sc[0], l_sc[0], acc_sc[1], l_sc[1], lam, sw_ref[...]).astype(o_ref.dtype)


def _attn_prompt(q, k, v, lam_rows, subln_w, n_heads, dk, dv):
    s_len = q.shape[0]
    t_blk = _tile(s_len, ATTN_TILE, LANES)
    nb = s_len // t_blk
    qi = np.concatenate([np.full(i + 1, i, np.int32) for i in range(nb)])
    kj = np.concatenate([np.arange(i + 1, dtype=np.int32) for i in range(nb)])
    rowspec = pl.BlockSpec((1, dk), lambda h, t, qi, kj: (0, 0))
    grid_spec = pltpu.PrefetchScalarGridSpec(
        num_scalar_prefetch=2,
        grid=(n_heads, len(qi)),
        in_specs=[
            pl.BlockSpec((t_blk, 2 * dk), lambda h, t, qi, kj: (qi[t], h)),
            pl.BlockSpec((t_blk, 2 * dk), lambda h, t, qi, kj: (kj[t], h)),
            pl.BlockSpec((t_blk, dv), lambda h, t, qi, kj: (kj[t], h)),
            rowspec, rowspec, rowspec, rowspec,
            pl.BlockSpec((1, dv), lambda h, t, qi, kj: (0, 0)),
        ],
        out_specs=pl.BlockSpec((t_blk, dv), lambda h, t, qi, kj: (qi[t], h)),
        scratch_shapes=[
            pltpu.VMEM((2, t_blk, 1), F32),
            pltpu.VMEM((2, t_blk, 1), F32),
            pltpu.VMEM((2, t_blk, dv), F32),
        ],
    )
    return pl.pallas_call(
        functools.partial(_attn_prompt_body, dk=dk, scale=dk ** -0.5),
        grid_spec=grid_spec,
        out_shape=jax.ShapeDtypeStruct((s_len, n_heads * dv), BF16),
        compiler_params=_params(("parallel", "arbitrary"), VMEM_LIMIT_BYTES),
        name="diff_attn_prompt",
    )(jnp.asarray(qi), jnp.asarray(kj), q, k, v, *lam_rows, subln_w)


def _attn_sample_body(pt_ref, q_ref, *refs, n_pg, n_steps, n_heads, dv, lp, l_new, scale):
    k_refs, v_refs = refs[:n_pg], refs[n_pg:2 * n_pg]
    kn_ref, vn_ref, lq1, lk1, lq2, lk2, sw_ref, o_ref, m_sc, l_sc, acc_sc = refs[2 * n_pg:]
    step = pl.program_id(1)

    @pl.when(step == 0)
    def _():
        m_sc[...] = jnp.full(m_sc.shape, -jnp.inf, F32)
        l_sc[...] = jnp.zeros(l_sc.shape, F32)
        acc_sc[...] = jnp.zeros(acc_sc.shape, F32)

    q = q_ref[...]

    def update(kb, vb, mask):
        s = _nt_dot(q, kb)
        if mask is not None:
            s = jnp.where(mask, s, -jnp.inf)
        m_new, alpha, p, l_next = _online_softmax_update(s, scale, m_sc[...], l_sc[...])
        acc_sc[...] = alpha * acc_sc[...] + jnp.dot(p.astype(BF16), vb, preferred_element_type=F32)
        m_sc[...] = m_new
        l_sc[...] = l_next

    @pl.when(step < n_steps)
    def _():
        for r in range(n_pg):
            update(k_refs[r][...].astype(BF16), v_refs[r][...].astype(BF16), None)

    @pl.when(step == n_steps)
    def _():
        shape = (q.shape[0], kn_ref.shape[0])
        row_i = jnp.bitwise_and(lax.broadcasted_iota(jnp.int32, shape, 0), lp - 1)
        col = lax.broadcasted_iota(jnp.int32, shape, 1)
        update(kn_ref[...], vn_ref[...], jnp.logical_and(col < l_new, col <= row_i))
        lam = _lambda_value(lq1, lk1, lq2, lk2)
        for h in range(n_heads):
            r0 = 2 * h * lp
            cols = slice(h * dv, (h + 1) * dv)
            o_ref[:, cols] = _diff_finish(
                acc_sc[r0:r0 + lp, cols], l_sc[r0:r0 + lp], acc_sc[r0 + lp:r0 + 2 * lp, cols],
                l_sc[r0 + lp:r0 + 2 * lp], lam, sw_ref[...])


def _attn_sample(q_rot, k_new, v_new, cache_k, cache_v, page_table, lam_rows, subln_w, n_heads, dk, dv):
    b, l_new, w_qk = q_rot.shape
    n_maps = 2 * n_heads
    n_pages = page_table.shape[1]
    page = cache_k.shape[1]
    w_v = n_heads * dv
    lp = max(SUBLANES, 1 << (l_new - 1).bit_length())
    assert l_new <= page
    qp = jnp.pad(q_rot.reshape(b, l_new, n_maps, dk), ((0, 0), (0, lp - l_new), (0, 0), (0, 0)))
    eye = jnp.eye(n_maps, dtype=BF16)
    q_bd = (qp.transpose(0, 2, 1, 3)[:, :, :, None, :] * eye[None, :, None, :, None]).reshape(b, n_maps * lp, w_qk)
    kn = jnp.pad(k_new, ((0, 0), (0, page - l_new), (0, 0)))
    vn = jnp.pad(v_new, ((0, 0), (0, page - l_new), (0, 0)))
    n_pg = _tile(n_pages, PAGES_PER_STEP, 1)
    n_steps = n_pages // n_pg

    def page_map(r):
        return lambda bi, s, pt: (pt[bi * n_pages + jnp.minimum(s, n_steps - 1) * n_pg + r], 0, 0)

    rowspec = pl.BlockSpec((1, dk), lambda bi, s, pt: (0, 0))
    seqspec = lambda rows, w: pl.BlockSpec((None, rows, w), lambda bi, s, pt: (bi, 0, 0))
    grid_spec = pltpu.PrefetchScalarGridSpec(
        num_scalar_prefetch=1,
        grid=(b, n_steps + 1),
        in_specs=(
            [seqspec(n_maps * lp, w_qk)]
            + [pl.BlockSpec((None, page, w_qk), page_map(r)) for r in range(n_pg)]
            + [pl.BlockSpec((None, page, w_v), page_map(r)) for r in range(n_pg)]
            + [seqspec(page, w_qk), seqspec(page, w_v), rowspec, rowspec, rowspec, rowspec,
               pl.BlockSpec((1, dv), lambda bi, s, pt: (0, 0))]
        ),
        out_specs=seqspec(lp, w_v),
        scratch_shapes=[
            pltpu.VMEM((n_maps * lp, 1), F32),
            pltpu.VMEM((n_maps * lp, 1), F32),
            pltpu.VMEM((n_maps * lp, w_v), F32),
        ],
    )
    out = pl.pallas_call(
        functools.partial(_attn_sample_body, n_pg=n_pg, n_steps=n_steps, n_heads=n_heads, dv=dv, lp=lp,
                          l_new=l_new, scale=dk ** -0.5),
        grid_spec=grid_spec,
        out_shape=jax.ShapeDtypeStruct((b, lp, w_v), F32),
        compiler_params=_params(("parallel", "arbitrary"), VMEM_LIMIT_BYTES),
        name="diff_attn_sample",
    )(page_table.reshape(-1), q_bd, *([cache_k] * n_pg), *([cache_v] * n_pg), kn, vn, *lam_rows, subln_w)
    return out[:, :l_new]


def _gla_constants(c):
    levels = c.bit_length() - 1
    assert 1 << levels == c
    idx = np.arange(c)
    i, t = idx[:, None], idx[None, :]
    sums = [t <= i, t > i]
    masks = [i == t]
    for lvl in range(1, levels + 1):
        blk, half = 1 << lvl, 1 << (lvl - 1)
        pos = idx % blk
        mid = (idx // blk) * blk + half - 1
        right, left = pos >= half, pos < half
        sums.append(right[:, None] & (t > mid[:, None]) & (t <= i))
        sums.append(left[:, None] & (t > i) & (t <= mid[:, None]))
        masks.append((idx[:, None] // blk == idx[None, :] // blk) & right[:, None] & left[None, :])
    return (jnp.asarray(np.concatenate(sums, 0).astype(np.float32), BF16),
            jnp.asarray(np.stack(masks, 0).astype(np.float32), F32))


def _gla_body(q_ref, k_ref, v_ref, r_ref, g_ref, s0_ref, sum_ref, mask_ref, gw_ref, o_ref, sout_ref, s_sc,
              *, c, levels, dk):
    ci = pl.program_id(2)

    @pl.when(ci == 0)
    def _():
        s_sc[...] = s0_ref[...].astype(F32)

    g = g_ref[...]
    g1 = g.astype(BF16)
    rem = g - g1.astype(F32)
    g2 = rem.astype(BF16)
    g3 = (rem - g2.astype(F32)).astype(BF16)
    sums = sum_ref[...]
    seg = (jnp.dot(sums, g1, preferred_element_type=F32) + jnp.dot(sums, g2, preferred_element_type=F32)
           + jnp.dot(sums, g3, preferred_element_type=F32))
    ones = jnp.ones((c, LANES), BF16)
    b_last = (_tn_dot(g1, ones) + _tn_dot(g2, ones) + _tn_dot(g3, ones))[:, 0:1]

    q = q_ref[...] * (dk ** -0.5)
    k = k_ref[...]
    v = v_ref[...].astype(BF16)
    state = s_sc[...]
    o = jnp.dot((q * jnp.exp(seg[0:c])).astype(BF16), state.astype(BF16), preferred_element_type=F32)
    a = mask_ref[0] * _nt_dot(q.astype(BF16), k.astype(BF16))
    for lvl in range(1, levels + 1):
        qd = (q * jnp.exp(seg[2 * lvl * c:(2 * lvl + 1) * c])).astype(BF16)
        kd = (k * jnp.exp(seg[(2 * lvl + 1) * c:(2 * lvl + 2) * c])).astype(BF16)
        a = a + mask_ref[lvl] * _nt_dot(qd, kd)
    o = o + jnp.dot(a.astype(BF16), v, preferred_element_type=F32)
    k_tail = (k * jnp.exp(seg[c:2 * c])).astype(BF16)
    new_state = state * jnp.exp(b_last) + _tn_dot(k_tail, v)
    s_sc[...] = new_state

    y = o * lax.rsqrt(jnp.mean(o * o, axis=-1, keepdims=True) + NORM_EPS) * gw_ref[...]
    r = r_ref[...]
    o_ref[...] = (y * (r * _sigmoid(r))).astype(o_ref.dtype)

    @pl.when(ci == pl.num_programs(2) - 1)
    def _():
        sout_ref[...] = new_state


def _gla(q, k, v, r, g, offs, s0, gla_norm_w, c):
    b, n_heads, dk, dv = s0.shape
    rows = g.shape[1]
    n_chunks = rows // c
    levels = c.bit_length() - 1
    sums, masks = _gla_constants(c)
    assert all(off % w == 0 for off, w in zip(offs, (dk, dk, dv, dv)))
    col = lambda w, off: pl.BlockSpec(
        (None, c, w), functools.partial(lambda bi, h, ci, o: (bi, ci, h + o), o=off // w))
    const3 = lambda shp: pl.BlockSpec(shp, lambda bi, h, ci: (0,) * len(shp))
    sspec = pl.BlockSpec((None, None, dk, dv), lambda bi, h, ci: (bi, h, 0, 0))
    return pl.pallas_call(
        functools.partial(_gla_body, c=c, levels=levels, dk=dk),
        grid=(b, n_heads, n_chunks),
        in_specs=[col(dk, offs[0]), col(dk, offs[1]), col(dv, offs[2]), col(dv, offs[3]), col(dk, 0), sspec,
                  const3(sums.shape), const3(masks.shape), const3((1, dv))],
        out_specs=[col(dv, 0), sspec],
        out_shape=[jax.ShapeDtypeStruct((b, rows, n_heads * dv), BF16),
                   jax.ShapeDtypeStruct((b, n_heads, dk, dv), F32)],
        scratch_shapes=[pltpu.VMEM((dk, dv), F32)],
        compiler_params=_params(("parallel", "parallel", "arbitrary"), VMEM_LIMIT_BYTES),
        name="gla",
    )(q, k, v, r, g, s0, sums, masks, gla_norm_w.reshape(1, dv).astype(F32))


def _router_body(x_ref, w_ref, wr_ref, br_ref, h_ref, idx_ref, gate_ref):
    x = x_ref[...]
    h = (x * lax.rsqrt(jnp.mean(x * x, axis=-1, keepdims=True) + NORM_EPS)) * w_ref[...]
    h_ref[...] = h
    logits = jnp.dot(h, wr_ref[...], precision=lax.Precision.HIGHEST, preferred_element_type=F32) + br_ref[...]
    lane = lax.broadcasted_iota(jnp.int32, logits.shape, 1)
    idx_out = jnp.zeros(logits.shape, jnp.int32)
    val_out = jnp.zeros(logits.shape, F32)
    top0 = None
    for kk in range(TOP_K):
        mx = jnp.max(logits, axis=-1, keepdims=True)
        idx = jnp.min(jnp.where(logits == mx, lane, LANES), axis=-1, keepdims=True)
        top0 = mx if top0 is None else top0
        idx_out = jnp.where(lane == kk, idx, idx_out)
        val_out = jnp.where(lane == kk, jnp.exp(mx - top0), val_out)
        logits = jnp.where(lane == idx, -jnp.inf, logits)
    idx_ref[...] = idx_out
    gate_ref[...] = val_out / jnp.sum(val_out, axis=-1, keepdims=True)


def _router(x, norm_w, w_router, b_router):
    n, d = x.shape
    e = w_router.shape[1]
    assert TOP_K <= e <= LANES
    wr = jnp.zeros((d, LANES), F32).at[:, :e].set(w_router.astype(F32))
    br = jnp.full((1, LANES), -jnp.inf, F32).at[0, :e].set(b_router.astype(F32))
    tr = _tile(n, ROW_TILE, SUBLANES)
    row = lambda w: pl.BlockSpec((tr, w), lambda i: (i, 0))
    fixed = lambda r, w: pl.BlockSpec((r, w), lambda i: (0, 0))
    return pl.pallas_call(
        _router_body,
        grid=(n // tr,),
        in_specs=[row(d), fixed(1, d), fixed(d, LANES), fixed(1, LANES)],
        out_specs=[row(d), row(LANES), row(LANES)],
        out_shape=[jax.ShapeDtypeStruct((n, d), F32), jax.ShapeDtypeStruct((n, LANES), jnp.int32),
                   jax.ShapeDtypeStruct((n, LANES), F32)],
        compiler_params=_params(("parallel",), VMEM_LIMIT_BYTES),
        name="ffn_norm_router",
    )(x, norm_w.reshape(1, d).astype(F32), wr, br)


def _gather_body(valid_ref, tok_ref, h_hbm, o_ref, buf, sem, *, rows):
    t = pl.program_id(0)

    def row_copy(src_row, r):
        return pltpu.make_async_copy(h_hbm.at[pl.ds(src_row, 1), :], buf.at[pl.ds(r, 1), :], sem)

    @pl.when(valid_ref[t] > 0)
    def _():
        def issue(r, carry):
            row_copy(tok_ref[0, r], r).start()
            return carry

        def wait(r, carry):
            row_copy(0, r).wait()
            return carry

        lax.fori_loop(0, rows, issue, 0)
        lax.fori_loop(0, rows, wait, 0)
        o_ref[...] = buf[...].astype(o_ref.dtype)

    @pl.when(valid_ref[t] == 0)
    def _():
        o_ref[...] = jnp.zeros(o_ref.shape, o_ref.dtype)


def _gather_rows(h, slot_tok, tile_valid, rows):
    n_slots = slot_tok.shape[0]
    d = h.shape[1]
    n_tiles = n_slots // rows
    grid_spec = pltpu.PrefetchScalarGridSpec(
        num_scalar_prefetch=1,
        grid=(n_tiles,),
        in_specs=[
            pl.BlockSpec((None, 1, rows), lambda t, valid: (t, 0, 0), memory_space=pltpu.SMEM),
            pl.BlockSpec(memory_space=pl.ANY),
        ],
        out_specs=pl.BlockSpec((rows, d), lambda t, valid: (t, 0)),
        scratch_shapes=[pltpu.VMEM((rows, d), F32), pltpu.SemaphoreType.DMA(())],
    )
    return pl.pallas_call(
        functools.partial(_gather_body, rows=rows),
        grid_spec=grid_spec,
        out_shape=jax.ShapeDtypeStruct((n_slots, d), BF16),
        compiler_params=_params(("arbitrary",), VMEM_LIMIT_BYTES),
        name="moe_gather",
    )(tile_valid, slot_tok.reshape(n_tiles, 1, rows), h)


def _expert_up_body(ge_ref, gr_ref, nv_ref, x_ref, wg_ref, wu_ref, bg_ref, bu_ref, o_ref, wg_sc, wu_sc, *, tm):
    g = pl.program_id(0)

    @pl.when(g < nv_ref[0])
    def _():
        wg_sc[...] = wg_ref[...].astype(BF16)
        wu_sc[...] = wu_ref[...].astype(BF16)
        n_tiles = (gr_ref[g] + tm - 1) // tm

        def tile(r, carry):
            rs = pl.ds(pl.multiple_of(r * tm, tm), tm)
            x = x_ref[rs, :]
            gt = jnp.dot(x, wg_sc[...], preferred_element_type=F32) + bg_ref[...]
            up = jnp.dot(x, wu_sc[...], preferred_element_type=F32) + bu_ref[...]
            gt = jnp.minimum(gt, SWIGLU_LIMIT)
            up = jnp.clip(up, -SWIGLU_LIMIT, SWIGLU_LIMIT)
            o_ref[rs, :] = ((up + 1.0) * (gt * _sigmoid(SWIGLU_ALPHA * gt))).astype(o_ref.dtype)
            return carry

        def zero(r, carry):
            o_ref[pl.ds(pl.multiple_of(r * tm, tm), tm), :] = jnp.zeros((tm, o_ref.shape[1]), o_ref.dtype)
            return carry

        lax.fori_loop(0, n_tiles, tile, 0)
        lax.fori_loop(n_tiles, x_ref.shape[0] // tm, zero, 0)


def _expert_down_body(ge_ref, gr_ref, nv_ref, x_ref, w_ref, b_ref, o_ref, w_sc, *, tm):
    g = pl.program_id(0)

    @pl.when(g < nv_ref[0])
    def _():
        w_sc[...] = w_ref[...].astype(BF16)
        n_tiles = (gr_ref[g] + tm - 1) // tm

        def tile(r, carry):
            rs = pl.ds(pl.multiple_of(r * tm, tm), tm)
            o_ref[rs, :] = jnp.dot(x_ref[rs, :], w_sc[...], preferred_element_type=F32) + b_ref[...]
            return carry

        def zero(r, carry):
            o_ref[pl.ds(pl.multiple_of(r * tm, tm), tm), :] = jnp.zeros((tm, o_ref.shape[1]), o_ref.dtype)
            return carry

        lax.fori_loop(0, n_tiles, tile, 0)
        lax.fori_loop(n_tiles, x_ref.shape[0] // tm, zero, 0)


def _expert_ffn(xs, group_expert, group_rows, n_valid, w_gate_up, b_gate_up, w_down, b_down, tm):
    n_groups, g_rows, d = xs.shape
    e, _, two_de = w_gate_up.shape
    de = two_de // 2
    tn_u = _tile(de, MOE_TN, LANES)
    tn_d = _tile(d, MOE_TN, LANES)
    nj_u, nj_d = de // tn_u, d // tn_d

    def maps(nj):
        gv = lambda g, nv: jnp.minimum(g, nv[0] - 1)
        jv = lambda g, j, nv: jnp.where(g < nv[0], j, nj - 1)
        return gv, jv

    gv, jv = maps(nj_u)
    act = pl.pallas_call(
        functools.partial(_expert_up_body, tm=tm),
        grid_spec=pltpu.PrefetchScalarGridSpec(
            num_scalar_prefetch=3,
            grid=(n_groups, nj_u),
            in_specs=[
                pl.BlockSpec((None, g_rows, d), lambda g, j, ge, gr, nv: (gv(g, nv), 0, 0)),
                pl.BlockSpec((None, d, tn_u), lambda g, j, ge, gr, nv: (ge[gv(g, nv)], 0, jv(g, j, nv))),
                pl.BlockSpec((None, d, tn_u), lambda g, j, ge, gr, nv: (ge[gv(g, nv)], 0, nj_u + jv(g, j, nv))),
                pl.BlockSpec((None, 1, tn_u), lambda g, j, ge, gr, nv: (ge[gv(g, nv)], 0, jv(g, j, nv))),
                pl.BlockSpec((None, 1, tn_u), lambda g, j, ge, gr, nv: (ge[gv(g, nv)], 0, nj_u + jv(g, j, nv))),
            ],
            out_specs=pl.BlockSpec((None, g_rows, tn_u), lambda g, j, ge, gr, nv: (gv(g, nv), 0, jv(g, j, nv))),
            scratch_shapes=[pltpu.VMEM((d, tn_u), BF16), pltpu.VMEM((d, tn_u), BF16)],
        ),
        out_shape=jax.ShapeDtypeStruct((n_groups, g_rows, de), BF16),
        compiler_params=_params(("arbitrary", "arbitrary"), VMEM_LIMIT_BYTES),
        name="moe_gate_up",
    )(group_expert, group_rows, n_valid, xs, w_gate_up, w_gate_up, b_gate_up.reshape(e, 1, two_de),
      b_gate_up.reshape(e, 1, two_de))

    gv, jv = maps(nj_d)
    return pl.pallas_call(
        functools.partial(_expert_down_body, tm=tm),
        grid_spec=pltpu.PrefetchScalarGridSpec(
            num_scalar_prefetch=3,
            grid=(n_groups, nj_d),
            in_specs=[
                pl.BlockSpec((None, g_rows, de), lambda g, j, ge, gr, nv: (gv(g, nv), 0, 0)),
                pl.BlockSpec((None, de, tn_d), lambda g, j, ge, gr, nv: (ge[gv(g, nv)], 0, jv(g, j, nv))),
                pl.BlockSpec((None, 1, tn_d), lambda g, j, ge, gr, nv: (ge[gv(g, nv)], 0, jv(g, j, nv))),
            ],
            out_specs=pl.BlockSpec((None, g_rows, tn_d), lambda g, j, ge, gr, nv: (gv(g, nv), 0, jv(g, j, nv))),
            scratch_shapes=[pltpu.VMEM((de, tn_d), BF16)],
        ),
        out_shape=jax.ShapeDtypeStruct((n_groups, g_rows, d), F32),
        compiler_params=_params(("arbitrary", "arbitrary"), VMEM_LIMIT_BYTES),
        name="moe_down",
    )(group_expert, group_rows, n_valid, act, w_down, b_down.reshape(e, 1, d))


def _combine_body(pos_ref, y_hbm, gate_ref, x_ref, w_ref, o_ref, buf, sem, *, rows):
    def row_copy(src_row, r):
        return pltpu.make_async_copy(y_hbm.at[pl.ds(src_row, 1), :], buf.at[pl.ds(r, 1), :], sem)

    def issue(r, carry):
        row_copy(pos_ref[0, r], r).start()
        return carry

    def wait(r, carry):
        row_copy(0, r).wait()
        return carry

    lax.fori_loop(0, TOP_K * rows, issue, 0)
    lax.fori_loop(0, TOP_K * rows, wait, 0)
    gates = gate_ref[...]
    x = x_ref[...]
    for kk in range(TOP_K):
        x = x + gates[:, kk:kk + 1] * buf[kk * rows:(kk + 1) * rows, :]
    y = x * lax.rsqrt(jnp.mean(x * x, axis=-1, keepdims=True) + NORM_EPS)
    o_ref[...] = y * w_ref[...]


def _combine(pos_tiles, y_slots, gates, x1, norm_w, row0, n_rows, rows):
    d = x1.shape[1]
    assert row0 % rows == 0 and n_rows % rows == 0
    rb = row0 // rows
    grid_spec = pltpu.PrefetchScalarGridSpec(
        num_scalar_prefetch=0,
        grid=(n_rows // rows,),
        in_specs=[
            pl.BlockSpec((None, 1, TOP_K * rows), lambda t: (t + rb, 0, 0), memory_space=pltpu.SMEM),
            pl.BlockSpec(memory_space=pl.ANY),
            pl.BlockSpec((rows, LANES), lambda t: (t + rb, 0)),
            pl.BlockSpec((rows, d), lambda t: (t + rb, 0)),
            pl.BlockSpec((1, d), lambda t: (0, 0)),
        ],
        out_specs=pl.BlockSpec((rows, d), lambda t: (t, 0)),
        scratch_shapes=[pltpu.VMEM((TOP_K * rows, d), F32), pltpu.SemaphoreType.DMA(())],
    )
    return pl.pallas_call(
        functools.partial(_combine_body, rows=rows),
        grid_spec=grid_spec,
        out_shape=jax.ShapeDtypeStruct((n_rows, d), F32),
        compiler_params=_params(("arbitrary",), VMEM_LIMIT_BYTES),
        name="moe_combine_norm",
    )(pos_tiles, y_slots, gates, x1, norm_w.reshape(1, d).astype(F32))


def _routing_tables(expert_ids, n_experts, g_rows, n_groups, gather_rows):
    n = expert_ids.shape[0]
    m = n * TOP_K
    e_flat = expert_ids.reshape(-1)
    onehot = (e_flat[:, None] == jnp.arange(n_experts, dtype=jnp.int32)[None, :]).astype(jnp.int32)
    before = jnp.cumsum(onehot, axis=0) - onehot
    rank = jnp.take_along_axis(before, e_flat[:, None], axis=1)[:, 0]
    counts = jnp.sum(onehot, axis=0)
    groups_per_e = (counts + g_rows - 1) // g_rows
    g_end = jnp.cumsum(groups_per_e)
    g_base = g_end - groups_per_e
    slot = (g_base[e_flat] + rank // g_rows) * g_rows + rank % g_rows
    n_valid = g_end[-1]
    gid = jnp.arange(n_groups, dtype=jnp.int32)
    g_exp = jnp.minimum(jnp.searchsorted(g_end, gid, side="right"), n_experts - 1).astype(jnp.int32)
    g_cnt = jnp.clip(counts[g_exp] - (gid - g_base[g_exp]) * g_rows, 0, g_rows)
    g_cnt = jnp.where(gid < n_valid, g_cnt, 0).astype(jnp.int32)
    slot_tok = jnp.zeros((n_groups * g_rows,), jnp.int32).at[slot].set(jnp.arange(m, dtype=jnp.int32) // TOP_K)
    tiles_per_g = g_rows // gather_rows
    tile_start = (jnp.arange(n_groups * tiles_per_g, dtype=jnp.int32) % tiles_per_g) * gather_rows
    tile_valid = (tile_start < jnp.repeat(g_cnt, tiles_per_g)).astype(jnp.int32)
    return slot.astype(jnp.int32), slot_tok, tile_valid, g_exp, g_cnt, n_valid.astype(jnp.int32).reshape(1)


def kernel(x_prompt, x_sample, cache_k, cache_v, state_gla, page_table, norm_mix_w, w_in, lambda_q1, lambda_k1, lambda_q2, lambda_k2, subln_w, w_alpha2, b_alpha, gla_norm_w, w_branch_a, w_branch_b, w_out, norm_ffn_w, w_router, b_router, w_gate_up, b_gate_up, w_down, b_down, norm_final_w):
    bp, s_len, d = x_prompt.shape
    bs, l_new, _ = x_sample.shape
    depth, n_pool, page, n_maps, dk_a = cache_k.shape
    _, _, _, h_a, dv_a = cache_v.shape
    _, _, h_b, dk_b, dv_b = state_gla.shape
    rank = w_alpha2.shape[1]
    n_experts = w_router.shape[2]
    assert depth == 1 and bp == 1 and n_maps == 2 * h_a
    n_p, n_s = bp * s_len, bs * l_new
    n_all = n_p + n_s
    sizes = (n_maps * dk_a, n_maps * dk_a, h_a * dv_a, h_b * dk_b, h_b * dk_b, h_b * dv_b, h_b * dv_b, d, d, rank)
    offs = [int(o) for o in np.cumsum((0,) + sizes)]
    n_main = offs[9]
    assert w_in.shape[2] == offs[10]
    past = page_table.shape[1] * page

    x_all = jnp.concatenate([x_prompt.reshape(n_p, d), x_sample.reshape(n_s, d)], axis=0)
    u = _rmsnorm(x_all, norm_mix_w[0], BF16)
    w_in_b = w_in[0].astype(BF16)
    z = _matmul([(u, w_in_b, 0)], [], n_main, _epi_plain, F32, "in_proj")
    g_all = _decay(u, w_in_b[:, n_main:], w_alpha2[0], b_alpha[0])

    lam_rows = [p[0].reshape(1, dk_a).astype(F32) for p in (lambda_q1, lambda_k1, lambda_q2, lambda_k2)]
    sw = subln_w[0].reshape(1, dv_a).astype(F32)

    pos_p = jnp.arange(s_len, dtype=jnp.int32)
    q_p, kf_p, kb_p, vf_p, vb_p = _prep(z, 0, n_p, pos_p, n_maps, dk_a, h_a * dv_a, offs[0:3])
    oa_p = _attn_prompt(q_p, kb_p, vb_p, lam_rows, sw, h_a, dk_a, dv_a)
    z3 = z.reshape(1, n_all, n_main)
    c_p = _tile(s_len, GLA_CHUNK, SUBLANES)
    ob_p, st_p = _gla(z3, z3, z3, z3, g_all.reshape(1, n_all, h_b * dk_b)[:, :n_p], offs[3:7],
                      jnp.zeros((bp, h_b, dk_b, dv_b), F32), gla_norm_w[0], c_p)

    pos_s = jnp.tile(past + jnp.arange(l_new, dtype=jnp.int32), bs)
    q_s, kf_s, kb_s, vf_s, vb_s = _prep(z, n_p, n_s, pos_s, n_maps, dk_a, h_a * dv_a, offs[0:3])
    oa_s = _attn_sample(
        q_s.reshape(bs, l_new, -1), kb_s.reshape(bs, l_new, -1), vb_s.reshape(bs, l_new, -1),
        cache_k[0].reshape(n_pool, page, n_maps * dk_a), cache_v[0].reshape(n_pool, page, h_a * dv_a),
        page_table, lam_rows, sw, h_a, dk_a, dv_a)
    c_s = max(SUBLANES, 1 << (l_new - 1).bit_length())
    pad_s = lambda a: jnp.pad(a.reshape(bs, l_new, -1), ((0, 0), (0, c_s - l_new), (0, 0)))
    zs = pad_s(z[n_p:, offs[3]:offs[7]])
    o3 = offs[3]
    ob_s, st_s = _gla(zs, zs, zs, zs, pad_s(g_all[n_p:]), [o - o3 for o in offs[3:7]],
                      state_gla[0].astype(F32), gla_norm_w[0], c_s)

    oa = jnp.concatenate([oa_p, oa_s.reshape(n_s, -1).astype(BF16)], axis=0)
    ob = jnp.concatenate([ob_p.reshape(n_p, -1), ob_s[:, :l_new].reshape(n_s, -1)], axis=0)
    mix = _matmul([(oa, w_branch_a[0].astype(BF16), 0), (ob, w_branch_b[0].astype(BF16), 0)],
                  [(z, offs[7]), (z, offs[8])], d, _epi_gated_merge, BF16, "branch_merge")
    x1 = _matmul([(mix, w_out[0].astype(BF16), 0)], [(x_all, 0)], d, _epi_residual, F32, "out_proj")

    h, top_i, gates = _router(x1, norm_ffn_w[0], w_router[0], b_router[0])
    m = n_all * TOP_K
    g_rows = -(-int(math.ceil(MOE_GROUP_SLACK * m / n_experts)) // MOE_TM) * MOE_TM
    gather_rows = _tile(g_rows, GATHER_ROWS, BF16_SUBLANES)
    n_groups = -(-m // g_rows) + n_experts
    slot, slot_tok, tile_valid, g_exp, g_cnt, n_valid = _routing_tables(
        top_i[:, :TOP_K], n_experts, g_rows, n_groups, gather_rows)
    xs = _gather_rows(h, slot_tok, tile_valid, gather_rows)
    y_slots = _expert_ffn(xs.reshape(n_groups, g_rows, d), g_exp, g_cnt, n_valid,
                          w_gate_up[0], b_gate_up[0], w_down[0], b_down[0], MOE_TM)
    y_slots = y_slots.reshape(n_groups * g_rows, d)
    rows_c = _tile(math.gcd(n_p, n_s), COMBINE_ROWS, SUBLANES)
    pos_tiles = slot.reshape(n_all // rows_c, rows_c, TOP_K).transpose(0, 2, 1).reshape(n_all // rows_c, 1, TOP_K * rows_c)
    y_p = _combine(pos_tiles, y_slots, gates, x1, norm_final_w, 0, n_p, rows_c)
    y_s = _combine(pos_tiles, y_slots, gates, x1, norm_final_w, n_p, n_s, rows_c)

    return (
        y_p.reshape(bp, s_len, d),
        y_s.reshape(bs, l_new, d),
        kf_p.reshape(1, bp, s_len, n_maps, dk_a),
        vf_p.reshape(1, bp, s_len, h_a, dv_a),
        st_p.reshape(1, bp, h_b, dk_b, dv_b),
        kf_s.reshape(1, bs, l_new, n_maps, dk_a),
        vf_s.reshape(1, bs, l_new, h_a, dv_a),
        st_s.reshape(1, bs, h_b, dk_b, dv_b),
    )
```

```python
import functools
import math

import jax
import jax.numpy as jnp
import numpy as np
from jax import lax
from jax.experimental import pallas as pl
from jax.experimental.pallas import tpu as pltpu

F32 = jnp.float32
BF16 = jnp.bfloat16

ROPE_THETA = 500000.0
GLA_TAU = 16.0
TOP_K = 4
SWIGLU_LIMIT = 7.0
SWIGLU_ALPHA = 1.702
NORM_EPS = 1e-5
LAMBDA_INIT = 0.8 - 0.6 * math.exp(-0.3 * 0)

LANES = 128
SUBLANES = 8
BF16_SUBLANES = 16
VMEM_LIMIT_BYTES = 56 * 1024 * 1024

ROW_TILE = 256
MM_TM = 1024
MM_TN = 512
ATTN_TILE = 512
PAGES_PER_STEP = 4
GLA_CHUNK = 128
MOE_TM = 128
MOE_TN = 256
MOE_TN_DOWN = 512
MOE_GROUP_SLACK = 1.2
GATHER_ROWS = 128
COMBINE_ROWS = 32


def _tile(n, pref, mult):
    best = None
    for d in range(mult, min(n, pref) + 1, mult):
        if n % d == 0:
            best = d
    return best if best is not None else n


def _params(sem, vmem=None):
    return pltpu.CompilerParams(dimension_semantics=sem, vmem_limit_bytes=vmem)


def _sigmoid(x):
    return 1.0 / (1.0 + jnp.exp(-x))


def _nt_dot(a, b):
    return lax.dot_general(a, b, (((1,), (1,)), ((), ())), preferred_element_type=F32)


def _tn_dot(a, b):
    return lax.dot_general(a, b, (((0,), (0,)), ((), ())), preferred_element_type=F32)


def _rmsnorm_body(x_ref, w_ref, o_ref):
    x = x_ref[...].astype(F32)
    y = x * lax.rsqrt(jnp.mean(x * x, axis=-1, keepdims=True) + NORM_EPS)
    o_ref[...] = (y * w_ref[...]).astype(o_ref.dtype)


def _rmsnorm(x, w, out_dtype):
    n, d = x.shape
    tr = _tile(n, ROW_TILE, BF16_SUBLANES)
    return pl.pallas_call(
        _rmsnorm_body,
        grid=(n // tr,),
        in_specs=[pl.BlockSpec((tr, d), lambda i: (i, 0)), pl.BlockSpec((1, d), lambda i: (0, 0))],
        out_specs=pl.BlockSpec((tr, d), lambda i: (i, 0)),
        out_shape=jax.ShapeDtypeStruct((n, d), out_dtype),
        compiler_params=_params(("parallel",)),
        name="rmsnorm",
    )(x, w.reshape(1, d).astype(F32))


def _mm_body(*refs, n_pairs, epilogue):
    o_ref = refs[-1]
    accs = [
        jnp.dot(refs[2 * p][...].astype(BF16), refs[2 * p + 1][...].astype(BF16), preferred_element_type=F32)
        for p in range(n_pairs)
    ]
    extras = [r[...] for r in refs[2 * n_pairs:-1]]
    o_ref[...] = epilogue(accs, extras).astype(o_ref.dtype)


def _matmul(pairs, extras, n_out, epilogue, out_dtype, name):
    m = pairs[0][0].shape[0]
    tm = _tile(m, MM_TM, BF16_SUBLANES)
    tn = _tile(n_out, MM_TN, LANES)
    in_specs, args = [], []
    for a, b, off in pairs:
        k = a.shape[1]
        assert off % tn == 0 and b.shape[0] == k
        in_specs.append(pl.BlockSpec((tm, k), lambda i, j: (i, 0)))
        in_specs.append(pl.BlockSpec((k, tn), functools.partial(lambda i, j, o: (0, j + o), o=off // tn)))
        args += [a, b]
    for e, off in extras:
        assert off % tn == 0
        in_specs.append(pl.BlockSpec((tm, tn), functools.partial(lambda i, j, o: (i, j + o), o=off // tn)))
        args.append(e)
    return pl.pallas_call(
        functools.partial(_mm_body, n_pairs=len(pairs), epilogue=epilogue),
        grid=(m // tm, n_out // tn),
        in_specs=in_specs,
        out_specs=pl.BlockSpec((tm, tn), lambda i, j: (i, j)),
        out_shape=jax.ShapeDtypeStruct((m, n_out), out_dtype),
        compiler_params=_params(("parallel", "arbitrary"), VMEM_LIMIT_BYTES),
        name=name,
    )(*args)


def _epi_plain(accs, extras):
    return accs[0]


def _epi_gated_merge(accs, extras):
    return _sigmoid(extras[0]) * accs[0] + _sigmoid(extras[1]) * accs[1]


def _epi_residual(accs, extras):
    return extras[0] + accs[0]


def _decay_body(u_ref, wlr_ref, wa2_ref, ba_ref, g_ref):
    a = jnp.dot(u_ref[...], wlr_ref[...], preferred_element_type=F32)
    x = jnp.dot(a.astype(BF16), wa2_ref[...], preferred_element_type=F32) + ba_ref[...]
    g_ref[...] = (jnp.minimum(x, 0.0) - jnp.log1p(jnp.exp(-jnp.abs(x)))) * (1.0 / GLA_TAU)


def _decay(u, w_lr, w_alpha2, b_alpha):
    n, d = u.shape
    rank, c = w_alpha2.shape
    assert rank <= LANES
    wlr = jnp.zeros((d, LANES), BF16).at[:, :rank].set(w_lr.astype(BF16))
    wa2 = jnp.zeros((LANES, c), BF16).at[:rank].set(w_alpha2.astype(BF16))
    tr = _tile(n, 2 * ROW_TILE, BF16_SUBLANES)
    return pl.pallas_call(
        _decay_body,
        grid=(n // tr,),
        in_specs=[
            pl.BlockSpec((tr, d), lambda i: (i, 0)),
            pl.BlockSpec((d, LANES), lambda i: (0, 0)),
            pl.BlockSpec((LANES, c), lambda i: (0, 0)),
            pl.BlockSpec((1, c), lambda i: (0, 0)),
        ],
        out_specs=pl.BlockSpec((tr, c), lambda i: (i, 0)),
        out_shape=jax.ShapeDtypeStruct((n, c), F32),
        compiler_params=_params(("parallel",)),
        name="gla_decay",
    )(u, wlr, wa2, b_alpha.reshape(1, c).astype(F32))


def _rope_tables(pos, dk):
    rot = dk // 4
    half = rot // 2
    inv_freq = jnp.power(jnp.float32(ROPE_THETA), -jnp.arange(half, dtype=F32) * (2.0 / rot))
    ang = pos.astype(F32)[:, None] * inv_freq[None, :]
    cos, sin = jnp.cos(ang), jnp.sin(ang)
    n = pos.shape[0]
    pad = jnp.zeros((n, dk - rot), F32)
    zero = jnp.zeros((n, half), F32)
    c = jnp.concatenate([cos, cos, pad + 1.0], axis=1)
    s1 = jnp.concatenate([-sin, zero, pad], axis=1)
    s2 = jnp.concatenate([zero, sin, pad], axis=1)
    return c, s1, s2


def _prep_body(zq_ref, zk_ref, zv_ref, c_ref, s1_ref, s2_ref, q_o, kf_o, kb_o, vf_o, vb_o, *, n_maps, dk):
    half = dk // 8
    c, s1, s2 = c_ref[...], s1_ref[...], s2_ref[...]
    q_scale = (dk ** -0.5) * math.log2(math.e)

    def rope(x):
        return x * c + pltpu.roll(x, dk - half, 1) * s1 + pltpu.roll(x, half, 1) * s2

    for m in range(n_maps):
        sl = slice(m * dk, (m + 1) * dk)
        q_o[:, sl] = (rope(zq_ref[:, sl]) * q_scale).astype(q_o.dtype)
        k = rope(zk_ref[:, sl])
        kf_o[:, sl] = k
        kb_o[:, sl] = k.astype(kb_o.dtype)
    v = zv_ref[...]
    vf_o[...] = v
    vb_o[...] = v.astype(vb_o.dtype)


def _prep(z, row0, n_rows, pos, n_maps, dk, w_v, offs):
    w_qk = n_maps * dk
    tr = _tile(n_rows, ROW_TILE, BF16_SUBLANES)
    assert row0 % tr == 0 and offs[0] % w_qk == 0 and offs[1] % w_qk == 0 and offs[2] % w_v == 0
    rb = row0 // tr
    c, s1, s2 = _rope_tables(pos, dk)
    zspec = lambda w, off: pl.BlockSpec((tr, w), functools.partial(lambda i, o: (i + rb, o), o=off // w))
    tspec = pl.BlockSpec((tr, dk), lambda i: (i, 0))
    ospec = lambda w: pl.BlockSpec((tr, w), lambda i: (i, 0))
    sds = lambda w, dt: jax.ShapeDtypeStruct((n_rows, w), dt)
    return pl.pallas_call(
        functools.partial(_prep_body, n_maps=n_maps, dk=dk),
        grid=(n_rows // tr,),
        in_specs=[zspec(w_qk, offs[0]), zspec(w_qk, offs[1]), zspec(w_v, offs[2]), tspec, tspec, tspec],
        out_specs=[ospec(w_qk), ospec(w_qk), ospec(w_qk), ospec(w_v), ospec(w_v)],
        out_shape=[sds(w_qk, BF16), sds(w_qk, F32), sds(w_qk, BF16), sds(w_v, F32), sds(w_v, BF16)],
        compiler_params=_params(("parallel",), VMEM_LIMIT_BYTES),
        name="qkv_rope",
    )(z, z, z, c, s1, s2)


def _lambda_value(lq1, lk1, lq2, lk2):
    a = jnp.exp(jnp.sum(lq1[...] * lk1[...], axis=-1, keepdims=True))
    b = jnp.exp(jnp.sum(lq2[...] * lk2[...], axis=-1, keepdims=True))
    return a - b + LAMBDA_INIT


def _diff_finish(a1, l1, a2, l2, lam, sw):
    o = a1 / l1 - lam * (a2 / l2)
    y = o * lax.rsqrt(jnp.mean(o * o, axis=-1, keepdims=True) + NORM_EPS)
    return (y * sw) * (1.0 - LAMBDA_INIT)


def _online_softmax_update(s, m_prev, l_prev):
    m_new = jnp.maximum(m_prev, jnp.max(s, axis=-1, keepdims=True))
    alpha = jnp.exp2(m_prev - m_new)
    p = jnp.exp2(s - m_new)
    return m_new, alpha, p, alpha * l_prev + jnp.sum(p, axis=-1, keepdims=True)


def _attn_prompt_body(qi_ref, kj_ref, q_ref, k_ref, v_ref, lq1, lk1, lq2, lk2, sw_ref, o_ref,
                      m_sc, l_sc, acc_sc, *, dk):
    t = pl.program_id(1)
    i = qi_ref[t]
    j = kj_ref[t]

    @pl.when(j == 0)
    def _():
        m_sc[...] = jnp.full(m_sc.shape, -jnp.inf, F32)
        l_sc[...] = jnp.zeros(l_sc.shape, F32)
        acc_sc[...] = jnp.zeros(acc_sc.shape, F32)

    def step(masked):
        v = v_ref[...]
        for c in range(2):
            s = _nt_dot(q_ref[:, c * dk:(c + 1) * dk], k_ref[:, c * dk:(c + 1) * dk])
            if masked:
                row = lax.broadcasted_iota(jnp.int32, s.shape, 0)
                col = lax.broadcasted_iota(jnp.int32, s.shape, 1)
                s = jnp.where(col <= row, s, -jnp.inf)
            m_new, alpha, p, l_new = _online_softmax_update(s, m_sc[c], l_sc[c])
            acc_sc[c] = alpha * acc_sc[c] + jnp.dot(p.astype(BF16), v, preferred_element_type=F32)
            m_sc[c] = m_new
            l_sc[c] = l_new

    @pl.when(j < i)
    def _():
        step(False)

    @pl.when(j == i)
    def _():
        step(True)
        lam = _lambda_value(lq1, lk1, lq2, lk2)
        o_ref[...] = _diff_finish(acc_sc[0], l_sc[0], acc_sc[1], l_sc[1], lam, sw_ref[...]).astype(o_ref.dtype)


def _attn_prompt(q, k, v, lam_rows, subln_w, n_heads, dk, dv):
    s_len = q.shape[0]
    t_blk = _tile(s_len, ATTN_TILE, LANES)
    nb = s_len // t_blk
    qi = np.concatenate([np.full(i + 1, i, np.int32) for i in range(nb)])
    kj = np.concatenate([np.arange(i + 1, dtype=np.int32) for i in range(nb)])
    rowspec = pl.BlockSpec((1, dk), lambda h, t, qi, kj: (0, 0))
    grid_spec = pltpu.PrefetchScalarGridSpec(
        num_scalar_prefetch=2,
        grid=(n_heads, len(qi)),
        in_specs=[
            pl.BlockSpec((t_blk, 2 * dk), lambda h, t, qi, kj: (qi[t], h)),
            pl.BlockSpec((t_blk, 2 * dk), lambda h, t, qi, kj: (kj[t], h)),
            pl.BlockSpec((t_blk, dv), lambda h, t, qi, kj: (kj[t], h)),
            rowspec, rowspec, rowspec, rowspec,
            pl.BlockSpec((1, dv), lambda h, t, qi, kj: (0, 0)),
        ],
        out_specs=pl.BlockSpec((t_blk, dv), lambda h, t, qi, kj: (qi[t], h)),
        scratch_shapes=[
            pltpu.VMEM((2, t_blk, 1), F32),
            pltpu.VMEM((2, t_blk, 1), F32),
            pltpu.VMEM((2, t_blk, dv), F32),
        ],
    )
    return pl.pallas_call(
        functools.partial(_attn_prompt_body, dk=dk),
        grid_spec=grid_spec,
        out_shape=jax.ShapeDtypeStruct((s_len, n_heads * dv), BF16),
        compiler_params=_params(("parallel", "arbitrary"), VMEM_LIMIT_BYTES),
        name="diff_attn_prompt",
    )(jnp.asarray(qi), jnp.asarray(kj), q, k, v, *lam_rows, subln_w)


def _attn_sample_body(pt_ref, q_ref, *refs, n_pg, n_steps, n_heads, dv, lp, l_new):
    n_vc = dv // LANES
    k_refs, v_refs = refs[:n_pg], refs[n_pg:n_pg * (1 + n_vc)]
    kn_ref, vn_ref, lq1, lk1, lq2, lk2, sw_ref, o_ref, m_sc, l_sc, acc_sc = refs[n_pg * (1 + n_vc):]
    step = pl.program_id(1)

    @pl.when(step == 0)
    def _():
        m_sc[...] = jnp.full(m_sc.shape, -jnp.inf, F32)
        l_sc[...] = jnp.zeros(l_sc.shape, F32)
        acc_sc[...] = jnp.zeros(acc_sc.shape, F32)

    q = q_ref[...]

    def update(kb, vb, mask):
        s = _nt_dot(q, kb)
        if mask is not None:
            s = jnp.where(mask, s, -jnp.inf)
        m_new, alpha, p, l_next = _online_softmax_update(s, m_sc[...], l_sc[...])
        acc_sc[...] = alpha * acc_sc[...] + jnp.dot(p.astype(BF16), vb, preferred_element_type=F32)
        m_sc[...] = m_new
        l_sc[...] = l_next

    @pl.when(step < n_steps)
    def _():
        def page_rows(refs_, groups):
            rows = refs_[0].shape[0] // groups
            return jnp.concatenate(
                [ref[pl.ds(gi, rows, stride=groups), :].astype(BF16) for gi in range(groups) for ref in refs_],
                axis=1)

        for r in range(n_pg):
            update(page_rows(k_refs[r:r + 1], 2 * n_heads), page_rows(v_refs[r * n_vc:(r + 1) * n_vc], n_heads), None)

    @pl.when(step == n_steps)
    def _():
        shape = (q.shape[0], kn_ref.shape[0])
        row_i = jnp.bitwise_and(lax.broadcasted_iota(jnp.int32, shape, 0), lp - 1)
        col = lax.broadcasted_iota(jnp.int32, shape, 1)
        update(kn_ref[...], vn_ref[...], jnp.logical_and(col < l_new, col <= row_i))
        lam = _lambda_value(lq1, lk1, lq2, lk2)
        for h in range(n_heads):
            r0 = 2 * h * lp
            cols = slice(h * dv, (h + 1) * dv)
            o_ref[:, cols] = _diff_finish(
                acc_sc[r0:r0 + lp, cols], l_sc[r0:r0 + lp], acc_sc[r0 + lp:r0 + 2 * lp, cols],
                l_sc[r0 + lp:r0 + 2 * lp], lam, sw_ref[...])


def _attn_sample(q_rot, k_new, v_new, cache_k, cache_v, page_table, lam_rows, subln_w, n_heads, dk, dv):
    b, l_new, w_qk = q_rot.shape
    n_maps = 2 * n_heads
    n_pages = page_table.shape[1]
    page = cache_v.shape[1] // n_heads
    w_v = n_heads * dv
    lp = max(SUBLANES, 1 << (l_new - 1).bit_length())
    assert l_new <= page
    qp = jnp.pad(q_rot.reshape(b, l_new, n_maps, dk), ((0, 0), (0, lp - l_new), (0, 0), (0, 0)))
    eye = jnp.eye(n_maps, dtype=BF16)
    q_bd = (qp.transpose(0, 2, 1, 3)[:, :, :, None, :] * eye[None, :, None, :, None]).reshape(b, n_maps * lp, w_qk)
    kn = jnp.pad(k_new, ((0, 0), (0, page - l_new), (0, 0)))
    vn = jnp.pad(v_new, ((0, 0), (0, page - l_new), (0, 0)))
    n_pg = _tile(n_pages, PAGES_PER_STEP, 1)
    n_steps = n_pages // n_pg

    assert dk == LANES and dv % LANES == 0
    n_vc = dv // LANES

    def page_map(r, cb=0):
        return lambda bi, s, pt: (pt[bi * n_pages + jnp.minimum(s, n_steps - 1) * n_pg + r], 0, cb)

    rowspec = pl.BlockSpec((1, dk), lambda bi, s, pt: (0, 0))
    seqspec = lambda rows, w: pl.BlockSpec((None, rows, w), lambda bi, s, pt: (bi, 0, 0))
    grid_spec = pltpu.PrefetchScalarGridSpec(
        num_scalar_prefetch=1,
        grid=(b, n_steps + 1),
        in_specs=(
            [seqspec(n_maps * lp, w_qk)]
            + [pl.BlockSpec((None, page * n_maps, dk), page_map(r)) for r in range(n_pg)]
            + [pl.BlockSpec((None, page * n_heads, LANES), page_map(r, cb))
               for r in range(n_pg) for cb in range(n_vc)]
            + [seqspec(page, w_qk), seqspec(page, w_v), rowspec, rowspec, rowspec, rowspec,
               pl.BlockSpec((1, dv), lambda bi, s, pt: (0, 0))]
        ),
        out_specs=seqspec(lp, w_v),
        scratch_shapes=[
            pltpu.VMEM((n_maps * lp, 1), F32),
            pltpu.VMEM((n_maps * lp, 1), F32),
            pltpu.VMEM((n_maps * lp, w_v), F32),
        ],
    )
    out = pl.pallas_call(
        functools.partial(_attn_sample_body, n_pg=n_pg, n_steps=n_steps, n_heads=n_heads, dv=dv, lp=lp,
                          l_new=l_new),
        grid_spec=grid_spec,
        out_shape=jax.ShapeDtypeStruct((b, lp, w_v), F32),
        compiler_params=_params(("parallel", "arbitrary"), VMEM_LIMIT_BYTES),
        name="diff_attn_sample",
    )(page_table.reshape(-1), q_bd, *([cache_k] * n_pg), *([cache_v] * (n_pg * n_vc)), kn, vn, *lam_rows, subln_w)
    return out[:, :l_new]


def _gla_constants(c):
    levels = c.bit_length() - 1
    assert 1 << levels == c
    idx = np.arange(c)
    i, t = idx[:, None], idx[None, :]
    sums = [t <= i, t > i]
    masks = [i == t]
    for lvl in range(1, levels + 1):
        blk, half = 1 << lvl, 1 << (lvl - 1)
        pos = idx % blk
        mid = (idx // blk) * blk + half - 1
        right, left = pos >= half, pos < half
        sums.append(right[:, None] & (t > mid[:, None]) & (t <= i))
        sums.append(left[:, None] & (t > i) & (t <= mid[:, None]))
        masks.append((idx[:, None] // blk == idx[None, :] // blk) & right[:, None] & left[None, :])
    return (jnp.asarray(np.concatenate(sums, 0).astype(np.float32), BF16),
            jnp.asarray(np.stack(masks, 0).astype(np.float32), F32))


def _gla_body(q_ref, k_ref, v_ref, r_ref, g_ref, s0_ref, sum_ref, mask_ref, gw_ref, o_ref, sout_ref, s_sc,
              *, c, levels, dk):
    ci = pl.program_id(2)

    @pl.when(ci == 0)
    def _():
        s_sc[...] = s0_ref[...].astype(F32)

    g = g_ref[...]
    g1 = g.astype(BF16)
    rem = g - g1.astype(F32)
    g2 = rem.astype(BF16)
    g3 = (rem - g2.astype(F32)).astype(BF16)
    sums = sum_ref[...]
    seg = (jnp.dot(sums, g1, preferred_element_type=F32) + jnp.dot(sums, g2, preferred_element_type=F32)
           + jnp.dot(sums, g3, preferred_element_type=F32))
    ones = jnp.ones((c, LANES), BF16)
    b_last = (_tn_dot(g1, ones) + _tn_dot(g2, ones) + _tn_dot(g3, ones))[:, 0:1]

    q = q_ref[...] * (dk ** -0.5)
    k = k_ref[...]
    v = v_ref[...].astype(BF16)
    state = s_sc[...]
    o = jnp.dot((q * jnp.exp(seg[0:c])).astype(BF16), state.astype(BF16), preferred_element_type=F32)
    a = mask_ref[0] * _nt_dot(q.astype(BF16), k.astype(BF16))
    for lvl in range(1, levels + 1):
        qd = (q * jnp.exp(seg[2 * lvl * c:(2 * lvl + 1) * c])).astype(BF16)
        kd = (k * jnp.exp(seg[(2 * lvl + 1) * c:(2 * lvl + 2) * c])).astype(BF16)
        a = a + mask_ref[lvl] * _nt_dot(qd, kd)
    o = o + jnp.dot(a.astype(BF16), v, preferred_element_type=F32)
    k_tail = (k * jnp.exp(seg[c:2 * c])).astype(BF16)
    new_state = state * jnp.exp(b_last) + _tn_dot(k_tail, v)
    s_sc[...] = new_state

    y = o * lax.rsqrt(jnp.mean(o * o, axis=-1, keepdims=True) + NORM_EPS) * gw_ref[...]
    r = r_ref[...]
    o_ref[...] = (y * (r * _sigmoid(r))).astype(o_ref.dtype)

    @pl.when(ci == pl.num_programs(2) - 1)
    def _():
        sout_ref[...] = new_state


def _gla(q, k, v, r, g, offs, s0, gla_norm_w, c):
    b, n_heads, dk, dv = s0.shape
    rows = g.shape[1]
    n_chunks = rows // c
    levels = c.bit_length() - 1
    sums, masks = _gla_constants(c)
    assert all(off % w == 0 for off, w in zip(offs, (dk, dk, dv, dv)))
    col = lambda w, off: pl.BlockSpec(
        (None, c, w), functools.partial(lambda bi, h, ci, o: (bi, ci, h + o), o=off // w))
    const3 = lambda shp: pl.BlockSpec(shp, lambda bi, h, ci: (0,) * len(shp))
    sspec = pl.BlockSpec((None, None, dk, dv), lambda bi, h, ci: (bi, h, 0, 0))
    return pl.pallas_call(
        functools.partial(_gla_body, c=c, levels=levels, dk=dk),
        grid=(b, n_heads, n_chunks),
        in_specs=[col(dk, offs[0]), col(dk, offs[1]), col(dv, offs[2]), col(dv, offs[3]), col(dk, 0), sspec,
                  const3(sums.shape), const3(masks.shape), const3((1, dv))],
        out_specs=[col(dv, 0), sspec],
        out_shape=[jax.ShapeDtypeStruct((b, rows, n_heads * dv), BF16),
                   jax.ShapeDtypeStruct((b, n_heads, dk, dv), F32)],
        scratch_shapes=[pltpu.VMEM((dk, dv), F32)],
        compiler_params=_params(("parallel", "parallel", "arbitrary"), VMEM_LIMIT_BYTES),
        name="gla",
    )(q, k, v, r, g, s0, sums, masks, gla_norm_w.reshape(1, dv).astype(F32))


def _router_body(x_ref, w_ref, wr_ref, br_ref, h_ref, idx_ref, gate_ref):
    x = x_ref[...]
    h = (x * lax.rsqrt(jnp.mean(x * x, axis=-1, keepdims=True) + NORM_EPS)) * w_ref[...]
    h_ref[...] = h
    logits = jnp.dot(h, wr_ref[...], precision=lax.Precision.HIGHEST, preferred_element_type=F32) + br_ref[...]
    lane = lax.broadcasted_iota(jnp.int32, logits.shape, 1)
    idx_out = jnp.zeros(logits.shape, jnp.int32)
    val_out = jnp.zeros(logits.shape, F32)
    top0 = None
    for kk in range(TOP_K):
        mx = jnp.max(logits, axis=-1, keepdims=True)
        idx = jnp.min(jnp.where(logits == mx, lane, LANES), axis=-1, keepdims=True)
        top0 = mx if top0 is None else top0
        idx_out = jnp.where(lane == kk, idx, idx_out)
        val_out = jnp.where(lane == kk, jnp.exp(mx - top0), val_out)
        logits = jnp.where(lane == idx, -jnp.inf, logits)
    idx_ref[...] = idx_out
    gate_ref[...] = val_out / jnp.sum(val_out, axis=-1, keepdims=True)


def _router(x, norm_w, w_router, b_router):
    n, d = x.shape
    e = w_router.shape[1]
    assert TOP_K <= e <= LANES
    wr = jnp.zeros((d, LANES), F32).at[:, :e].set(w_router.astype(F32))
    br = jnp.full((1, LANES), -jnp.inf, F32).at[0, :e].set(b_router.astype(F32))
    tr = _tile(n, ROW_TILE, SUBLANES)
    row = lambda w: pl.BlockSpec((tr, w), lambda i: (i, 0))
    fixed = lambda r, w: pl.BlockSpec((r, w), lambda i: (0, 0))
    return pl.pallas_call(
        _router_body,
        grid=(n // tr,),
        in_specs=[row(d), fixed(1, d), fixed(d, LANES), fixed(1, LANES)],
        out_specs=[row(d), row(LANES), row(LANES)],
        out_shape=[jax.ShapeDtypeStruct((n, d), F32), jax.ShapeDtypeStruct((n, LANES), jnp.int32),
                   jax.ShapeDtypeStruct((n, LANES), F32)],
        compiler_params=_params(("parallel",), VMEM_LIMIT_BYTES),
        name="ffn_norm_router",
    )(x, norm_w.reshape(1, d).astype(F32), wr, br)


def _gather_body(valid_ref, tok_ref, h_hbm, o_ref, buf, sem, *, rows):
    t = pl.program_id(0)

    def row_copy(src_row, r):
        return pltpu.make_async_copy(h_hbm.at[pl.ds(src_row, 1), :], buf.at[pl.ds(r, 1), :], sem)

    @pl.when(valid_ref[t] > 0)
    def _():
        def issue(r, carry):
            row_copy(tok_ref[0, r], r).start()
            return carry

        def wait(r, carry):
            row_copy(0, r).wait()
            return carry

        lax.fori_loop(0, rows, issue, 0)
        lax.fori_loop(0, rows, wait, 0)
        o_ref[...] = buf[...].astype(o_ref.dtype)

    @pl.when(valid_ref[t] == 0)
    def _():
        o_ref[...] = jnp.zeros(o_ref.shape, o_ref.dtype)


def _gather_rows(h, slot_tok, tile_valid, rows):
    n_slots = slot_tok.shape[0]
    d = h.shape[1]
    n_tiles = n_slots // rows
    grid_spec = pltpu.PrefetchScalarGridSpec(
        num_scalar_prefetch=1,
        grid=(n_tiles,),
        in_specs=[
            pl.BlockSpec((None, 1, rows), lambda t, valid: (t, 0, 0), memory_space=pltpu.SMEM),
            pl.BlockSpec(memory_space=pl.ANY),
        ],
        out_specs=pl.BlockSpec((rows, d), lambda t, valid: (t, 0)),
        scratch_shapes=[pltpu.VMEM((rows, d), F32), pltpu.SemaphoreType.DMA(())],
    )
    return pl.pallas_call(
        functools.partial(_gather_body, rows=rows),
        grid_spec=grid_spec,
        out_shape=jax.ShapeDtypeStruct((n_slots, d), BF16),
        compiler_params=_params(("arbitrary",), VMEM_LIMIT_BYTES),
        name="moe_gather",
    )(tile_valid, slot_tok.reshape(n_tiles, 1, rows), h)


def _for_row_tiles(n_rows, o_ref, tm, tile_fn):
    n_small = (n_rows + tm - 1) // tm
    n_big = n_small // 2

    def big_tile(r, carry):
        tile_fn(pl.multiple_of(r * (2 * tm), 2 * tm), 2 * tm)
        return carry

    def zero(r, carry):
        o_ref[pl.ds(pl.multiple_of(r * tm, tm), tm), :] = jnp.zeros((tm, o_ref.shape[1]), o_ref.dtype)
        return carry

    lax.fori_loop(0, n_big, big_tile, 0)

    @pl.when(n_small % 2 == 1)
    def _():
        tile_fn(pl.multiple_of(n_big * (2 * tm), tm), tm)

    lax.fori_loop(n_small, o_ref.shape[0] // tm, zero, 0)


def _expert_up_body(ge_ref, gr_ref, nv_ref, x_ref, wg_ref, wu_ref, bg_ref, bu_ref, o_ref, w_sc, *, tm):
    g = pl.program_id(0)
    tn = o_ref.shape[1]

    @pl.when(g < nv_ref[0])
    def _():
        w_sc[:, :tn] = wg_ref[...].astype(BF16)
        w_sc[:, tn:] = wu_ref[...].astype(BF16)
        bias = jnp.concatenate([bg_ref[...], bu_ref[...]], axis=1)

        def tile(start, size):
            rs = pl.ds(start, size)
            gu = jnp.dot(x_ref[rs, :], w_sc[...], preferred_element_type=F32) + bias
            gt = jnp.minimum(gu[:, :tn], SWIGLU_LIMIT)
            up = jnp.clip(gu[:, tn:], -SWIGLU_LIMIT, SWIGLU_LIMIT)
            o_ref[rs, :] = ((up + 1.0) * (gt * _sigmoid(SWIGLU_ALPHA * gt))).astype(o_ref.dtype)

        _for_row_tiles(gr_ref[g], o_ref, tm, tile)


def _expert_down_body(ge_ref, gr_ref, nv_ref, x_ref, w_ref, b_ref, o_ref, w_sc, *, tm):
    g = pl.program_id(0)

    @pl.when(g < nv_ref[0])
    def _():
        w_sc[...] = w_ref[...].astype(BF16)

        def tile(start, size):
            rs = pl.ds(start, size)
            o_ref[rs, :] = jnp.dot(x_ref[rs, :], w_sc[...], preferred_element_type=F32) + b_ref[...]

        _for_row_tiles(gr_ref[g], o_ref, tm, tile)


def _expert_ffn(xs, group_expert, group_rows, n_valid, w_gate_up, b_gate_up, w_down, b_down, tm):
    n_groups, g_rows, d = xs.shape
    e, _, two_de = w_gate_up.shape
    de = two_de // 2
    tn_u = _tile(de, MOE_TN, LANES)
    tn_d = _tile(d, MOE_TN_DOWN, LANES)
    nj_u, nj_d = de // tn_u, d // tn_d

    def maps(nj):
        gv = lambda g, nv: jnp.minimum(g, nv[0] - 1)
        jv = lambda g, j, nv: jnp.where(g < nv[0], j, nj - 1)
        return gv, jv

    gv, jv = maps(nj_u)
    act = pl.pallas_call(
        functools.partial(_expert_up_body, tm=tm),
        grid_spec=pltpu.PrefetchScalarGridSpec(
            num_scalar_prefetch=3,
            grid=(n_groups, nj_u),
            in_specs=[
                pl.BlockSpec((None, g_rows, d), lambda g, j, ge, gr, nv: (gv(g, nv), 0, 0)),
                pl.BlockSpec((None, d, tn_u), lambda g, j, ge, gr, nv: (ge[gv(g, nv)], 0, jv(g, j, nv))),
                pl.BlockSpec((None, d, tn_u), lambda g, j, ge, gr, nv: (ge[gv(g, nv)], 0, nj_u + jv(g, j, nv))),
                pl.BlockSpec((None, 1, tn_u), lambda g, j, ge, gr, nv: (ge[gv(g, nv)], 0, jv(g, j, nv))),
                pl.BlockSpec((None, 1, tn_u), lambda g, j, ge, gr, nv: (ge[gv(g, nv)], 0, nj_u + jv(g, j, nv))),
            ],
            out_specs=pl.BlockSpec((None, g_rows, tn_u), lambda g, j, ge, gr, nv: (gv(g, nv), 0, jv(g, j, nv))),
            scratch_shapes=[pltpu.VMEM((d, 2 * tn_u), BF16)],
        ),
        out_shape=jax.ShapeDtypeStruct((n_groups, g_rows, de), BF16),
        compiler_params=_params(("arbitrary", "arbitrary"), VMEM_LIMIT_BYTES),
        name="moe_gate_up",
    )(group_expert, group_rows, n_valid, xs, w_gate_up, w_gate_up, b_gate_up.reshape(e, 1, two_de),
      b_gate_up.reshape(e, 1, two_de))

    gv, jv = maps(nj_d)
    return pl.pallas_call(
        functools.partial(_expert_down_body, tm=tm),
        grid_spec=pltpu.PrefetchScalarGridSpec(
            num_scalar_prefetch=3,
            grid=(n_groups, nj_d),
            in_specs=[
                pl.BlockSpec((None, g_rows, de), lambda g, j, ge, gr, nv: (gv(g, nv), 0, 0)),
                pl.BlockSpec((None, de, tn_d), lambda g, j, ge, gr, nv: (ge[gv(g, nv)], 0, jv(g, j, nv))),
                pl.BlockSpec((None, 1, tn_d), lambda g, j, ge, gr, nv: (ge[gv(g, nv)], 0, jv(g, j, nv))),
            ],
            out_specs=pl.BlockSpec((None, g_rows, tn_d), lambda g, j, ge, gr, nv: (gv(g, nv), 0, jv(g, j, nv))),
            scratch_shapes=[pltpu.VMEM((de, tn_d), BF16)],
        ),
        out_shape=jax.ShapeDtypeStruct((n_groups, g_rows, d), F32),
        compiler_params=_params(("arbitrary", "arbitrary"), VMEM_LIMIT_BYTES),
        name="moe_down",
    )(group_expert, group_rows, n_valid, act, w_down, b_down.reshape(e, 1, d))


def _combine_body(pos_ref, y_hbm, gate_ref, x_ref, w_ref, o_ref, buf, sem, *, rows):
    def row_copy(src_row, r):
        return pltpu.make_async_copy(y_hbm.at[pl.ds(src_row, 1), :], buf.at[pl.ds(r, 1), :], sem)

    def issue(r, carry):
        row_copy(pos_ref[0, r], r).start()
        return carry

    def wait(r, carry):
        row_copy(0, r).wait()
        return carry

    lax.fori_loop(0, TOP_K * rows, issue, 0)
    lax.fori_loop(0, TOP_K * rows, wait, 0)
    gates = gate_ref[...]
    x = x_ref[...]
    for kk in range(TOP_K):
        x = x + gates[:, kk:kk + 1] * buf[kk * rows:(kk + 1) * rows, :]
    y = x * lax.rsqrt(jnp.mean(x * x, axis=-1, keepdims=True) + NORM_EPS)
    o_ref[...] = y * w_ref[...]


def _combine(pos_tiles, y_slots, gates, x1, norm_w, row0, n_rows, rows):
    d = x1.shape[1]
    assert row0 % rows == 0 and n_rows % rows == 0
    rb = row0 // rows
    grid_spec = pltpu.PrefetchScalarGridSpec(
        num_scalar_prefetch=0,
        grid=(n_rows // rows,),
        in_specs=[
            pl.BlockSpec((None, 1, TOP_K * rows), lambda t: (t + rb, 0, 0), memory_space=pltpu.SMEM),
            pl.BlockSpec(memory_space=pl.ANY),
            pl.BlockSpec((rows, LANES), lambda t: (t + rb, 0)),
            pl.BlockSpec((rows, d), lambda t: (t + rb, 0)),
            pl.BlockSpec((1, d), lambda t: (0, 0)),
        ],
        out_specs=pl.BlockSpec((rows, d), lambda t: (t, 0)),
        scratch_shapes=[pltpu.VMEM((TOP_K * rows, d), F32), pltpu.SemaphoreType.DMA(())],
    )
    return pl.pallas_call(
        functools.partial(_combine_body, rows=rows),
        grid_spec=grid_spec,
        out_shape=jax.ShapeDtypeStruct((n_rows, d), F32),
        compiler_params=_params(("arbitrary",), VMEM_LIMIT_BYTES),
        name="moe_combine_norm",
    )(pos_tiles, y_slots, gates, x1, norm_w.reshape(1, d).astype(F32))


def _routing_tables(expert_ids, n_experts, g_rows, n_groups, gather_rows):
    n = expert_ids.shape[0]
    m = n * TOP_K
    e_flat = expert_ids.reshape(-1)
    onehot = (e_flat[:, None] == jnp.arange(n_experts, dtype=jnp.int32)[None, :]).astype(jnp.int32)
    before = jnp.cumsum(onehot, axis=0) - onehot
    rank = jnp.take_along_axis(before, e_flat[:, None], axis=1)[:, 0]
    counts = jnp.sum(onehot, axis=0)
    groups_per_e = (counts + g_rows - 1) // g_rows
    g_end = jnp.cumsum(groups_per_e)
    g_base = g_end - groups_per_e
    slot = (g_base[e_flat] + rank // g_rows) * g_rows + rank % g_rows
    n_valid = g_end[-1]
    gid = jnp.arange(n_groups, dtype=jnp.int32)
    g_exp = jnp.minimum(jnp.searchsorted(g_end, gid, side="right"), n_experts - 1).astype(jnp.int32)
    g_cnt = jnp.clip(counts[g_exp] - (gid - g_base[g_exp]) * g_rows, 0, g_rows)
    g_cnt = jnp.where(gid < n_valid, g_cnt, 0).astype(jnp.int32)
    slot_tok = jnp.zeros((n_groups * g_rows,), jnp.int32).at[slot].set(jnp.arange(m, dtype=jnp.int32) // TOP_K)
    tiles_per_g = g_rows // gather_rows
    tile_start = (jnp.arange(n_groups * tiles_per_g, dtype=jnp.int32) % tiles_per_g) * gather_rows
    tile_valid = (tile_start < jnp.repeat(g_cnt, tiles_per_g)).astype(jnp.int32)
    return slot.astype(jnp.int32), slot_tok, tile_valid, g_exp, g_cnt, n_valid.astype(jnp.int32).reshape(1)


def kernel(x_prompt, x_sample, cache_k, cache_v, state_gla, page_table, norm_mix_w, w_in, lambda_q1, lambda_k1, lambda_q2, lambda_k2, subln_w, w_alpha2, b_alpha, gla_norm_w, w_branch_a, w_branch_b, w_out, norm_ffn_w, w_router, b_router, w_gate_up, b_gate_up, w_down, b_down, norm_final_w):
    bp, s_len, d = x_prompt.shape
    bs, l_new, _ = x_sample.shape
    depth, n_pool, page, n_maps, dk_a = cache_k.shape
    _, _, _, h_a, dv_a = cache_v.shape
    _, _, h_b, dk_b, dv_b = state_gla.shape
    rank = w_alpha2.shape[1]
    n_experts = w_router.shape[2]
    assert depth == 1 and bp == 1 and n_maps == 2 * h_a
    n_p, n_s = bp * s_len, bs * l_new
    n_all = n_p + n_s
    sizes = (n_maps * dk_a, n_maps * dk_a, h_a * dv_a, h_b * dk_b, h_b * dk_b, h_b * dv_b, h_b * dv_b, d, d, rank)
    offs = [int(o) for o in np.cumsum((0,) + sizes)]
    n_main = offs[9]
    assert w_in.shape[2] == offs[10]
    past = page_table.shape[1] * page

    x_all = jnp.concatenate([x_prompt.reshape(n_p, d), x_sample.reshape(n_s, d)], axis=0)
    u = _rmsnorm(x_all, norm_mix_w[0], BF16)
    w_in_b = w_in[0].astype(BF16)
    z = _matmul([(u, w_in_b, 0)], [], n_main, _epi_plain, F32, "in_proj")
    g_all = _decay(u, w_in_b[:, n_main:], w_alpha2[0], b_alpha[0])

    lam_rows = [p[0].reshape(1, dk_a).astype(F32) for p in (lambda_q1, lambda_k1, lambda_q2, lambda_k2)]
    sw = subln_w[0].reshape(1, dv_a).astype(F32)

    pos_p = jnp.arange(s_len, dtype=jnp.int32)
    q_p, kf_p, kb_p, vf_p, vb_p = _prep(z, 0, n_p, pos_p, n_maps, dk_a, h_a * dv_a, offs[0:3])
    oa_p = _attn_prompt(q_p, kb_p, vb_p, lam_rows, sw, h_a, dk_a, dv_a)
    z3 = z.reshape(1, n_all, n_main)
    c_p = _tile(s_len, GLA_CHUNK, SUBLANES)
    ob_p, st_p = _gla(z3, z3, z3, z3, g_all.reshape(1, n_all, h_b * dk_b)[:, :n_p], offs[3:7],
                      jnp.zeros((bp, h_b, dk_b, dv_b), F32), gla_norm_w[0], c_p)

    pos_s = jnp.tile(past + jnp.arange(l_new, dtype=jnp.int32), bs)
    q_s, kf_s, kb_s, vf_s, vb_s = _prep(z, n_p, n_s, pos_s, n_maps, dk_a, h_a * dv_a, offs[0:3])
    oa_s = _attn_sample(
        q_s.reshape(bs, l_new, -1), kb_s.reshape(bs, l_new, -1), vb_s.reshape(bs, l_new, -1),
        cache_k.reshape(n_pool, page * n_maps, dk_a), cache_v.reshape(n_pool, page * h_a, dv_a),
        page_table, lam_rows, sw, h_a, dk_a, dv_a)
    c_s = max(SUBLANES, 1 << (l_new - 1).bit_length())
    pad_s = lambda a: jnp.pad(a.reshape(bs, l_new, -1), ((0, 0), (0, c_s - l_new), (0, 0)))
    zs = pad_s(z[n_p:, offs[3]:offs[7]])
    o3 = offs[3]
    ob_s, st_s = _gla(zs, zs, zs, zs, pad_s(g_all[n_p:]), [o - o3 for o in offs[3:7]],
                      state_gla[0].astype(F32), gla_norm_w[0], c_s)

    oa = jnp.concatenate([oa_p, oa_s.reshape(n_s, -1).astype(BF16)], axis=0)
    ob = jnp.concatenate([ob_p.reshape(n_p, -1), ob_s[:, :l_new].reshape(n_s, -1)], axis=0)
    mix = _matmul([(oa, w_branch_a[0].astype(BF16), 0), (ob, w_branch_b[0].astype(BF16), 0)],
                  [(z, offs[7]), (z, offs[8])], d, _epi_gated_merge, BF16, "branch_merge")
    x1 = _matmul([(mix, w_out[0].astype(BF16), 0)], [(x_all, 0)], d, _epi_residual, F32, "out_proj")

    h, top_i, gates = _router(x1, norm_ffn_w[0], w_router[0], b_router[0])
    m = n_all * TOP_K
    g_rows = -(-int(math.ceil(MOE_GROUP_SLACK * m / n_experts)) // MOE_TM) * MOE_TM
    gather_rows = _tile(g_rows, GATHER_ROWS, BF16_SUBLANES)
    n_groups = -(-m // g_rows) + n_experts
    slot, slot_tok, tile_valid, g_exp, g_cnt, n_valid = _routing_tables(
        top_i[:, :TOP_K], n_experts, g_rows, n_groups, gather_rows)
    xs = _gather_rows(h, slot_tok, tile_valid, gather_rows)
    y_slots = _expert_ffn(xs.reshape(n_groups, g_rows, d), g_exp, g_cnt, n_valid,
                          w_gate_up[0], b_gate_up[0], w_down[0], b_down[0], MOE_TM)
    y_slots = y_slots.reshape(n_groups * g_rows, d)
    rows_c = _tile(math.gcd(n_p, n_s), COMBINE_ROWS, SUBLANES)
    pos_tiles = slot.reshape(n_all // rows_c, rows_c, TOP_K).transpose(0, 2, 1).reshape(n_all // rows_c, 1, TOP_K * rows_c)
    y_p = _combine(pos_tiles, y_slots, gates, x1, norm_final_w, 0, n_p, rows_c)
    y_s = _combine(pos_tiles, y_slots, gates, x1, norm_final_w, n_p, n_s, rows_c)

    return (
        y_p.reshape(bp, s_len, d),
        y_s.reshape(bs, l_new, d),
        kf_p.reshape(1, bp, s_len, n_maps, dk_a),
        vf_p.reshape(1, bp, s_len, h_a, dv_a),
        st_p.reshape(1, bp, h_b, dk_b, dv_b),
        kf_s.reshape(1, bs, l_new, n_maps, dk_a),
        vf_s.reshape(1, bs, l_new, h_a, dv_a),
        st_s.reshape(1, bs, h_b, dk_b, dv_b),
    )
```

```python
import functools
import math

import jax
import jax.numpy as jnp
import numpy as np
from jax import lax
from jax.experimental import pallas as pl
from jax.experimental.pallas import tpu as pltpu

F32 = jnp.float32
BF16 = jnp.bfloat16

ROPE_THETA = 500000.0
GLA_TAU = 16.0
TOP_K = 4
SWIGLU_LIMIT = 7.0
SWIGLU_ALPHA = 1.702
NORM_EPS = 1e-5
LAMBDA_INIT = 0.8 - 0.6 * math.exp(-0.3 * 0)

LANES = 128
SUBLANES = 8
BF16_SUBLANES = 16
VMEM_LIMIT_BYTES = 56 * 1024 * 1024

ROW_TILE = 256
MM_TM = 1024
MM_TN = 512
ATTN_TILE = 1024
ATTN_ROW_BLOCK = 128
PAGES_PER_STEP = 4
GLA_CHUNK = 128
MOE_TM = 128
MOE_TN = 256
MOE_TN_DOWN = 512
MOE_GROUP_SLACK = 1.2
GATHER_ROWS = 128
COMBINE_ROWS = 32


def _tile(n, pref, mult):
    best = None
    for d in range(mult, min(n, pref) + 1, mult):
        if n % d == 0:
            best = d
    return best if best is not None else n


def _params(sem, vmem=None):
    return pltpu.CompilerParams(dimension_semantics=sem, vmem_limit_bytes=vmem)


def _sigmoid(x):
    return 1.0 / (1.0 + jnp.exp(-x))


def _nt_dot(a, b):
    return lax.dot_general(a, b, (((1,), (1,)), ((), ())), preferred_element_type=F32)


def _tn_dot(a, b):
    return lax.dot_general(a, b, (((0,), (0,)), ((), ())), preferred_element_type=F32)


def _rmsnorm_body(x_ref, w_ref, o_ref):
    x = x_ref[...].astype(F32)
    y = x * lax.rsqrt(jnp.mean(x * x, axis=-1, keepdims=True) + NORM_EPS)
    o_ref[...] = (y * w_ref[...]).astype(o_ref.dtype)


def _rmsnorm(x, w, out_dtype):
    n, d = x.shape
    tr = _tile(n, ROW_TILE, BF16_SUBLANES)
    return pl.pallas_call(
        _rmsnorm_body,
        grid=(n // tr,),
        in_specs=[pl.BlockSpec((tr, d), lambda i: (i, 0)), pl.BlockSpec((1, d), lambda i: (0, 0))],
        out_specs=pl.BlockSpec((tr, d), lambda i: (i, 0)),
        out_shape=jax.ShapeDtypeStruct((n, d), out_dtype),
        compiler_params=_params(("parallel",)),
        name="rmsnorm",
    )(x, w.reshape(1, d).astype(F32))


def _mm_body(*refs, n_pairs, epilogue):
    o_ref = refs[-1]
    accs = [
        jnp.dot(refs[2 * p][...].astype(BF16), refs[2 * p + 1][...].astype(BF16), preferred_element_type=F32)
        for p in range(n_pairs)
    ]
    extras = [r[...] for r in refs[2 * n_pairs:-1]]
    o_ref[...] = epilogue(accs, extras).astype(o_ref.dtype)


def _matmul(pairs, extras, n_out, epilogue, out_dtype, name):
    m = pairs[0][0].shape[0]
    tm = _tile(m, MM_TM, BF16_SUBLANES)
    tn = _tile(n_out, MM_TN, LANES)
    in_specs, args = [], []
    for a, b, off in pairs:
        k = a.shape[1]
        assert off % tn == 0 and b.shape[0] == k
        in_specs.append(pl.BlockSpec((tm, k), lambda i, j: (i, 0)))
        in_specs.append(pl.BlockSpec((k, tn), functools.partial(lambda i, j, o: (0, j + o), o=off // tn)))
        args += [a, b]
    for e, off in extras:
        assert off % tn == 0
        in_specs.append(pl.BlockSpec((tm, tn), functools.partial(lambda i, j, o: (i, j + o), o=off // tn)))
        args.append(e)
    return pl.pallas_call(
        functools.partial(_mm_body, n_pairs=len(pairs), epilogue=epilogue),
        grid=(m // tm, n_out // tn),
        in_specs=in_specs,
        out_specs=pl.BlockSpec((tm, tn), lambda i, j: (i, j)),
        out_shape=jax.ShapeDtypeStruct((m, n_out), out_dtype),
        compiler_params=_params(("parallel", "arbitrary"), VMEM_LIMIT_BYTES),
        name=name,
    )(*args)


def _epi_plain(accs, extras):
    return accs[0]


def _epi_gated_merge(accs, extras):
    return _sigmoid(extras[0]) * accs[0] + _sigmoid(extras[1]) * accs[1]


def _epi_residual(accs, extras):
    return extras[0] + accs[0]


def _decay_body(u_ref, wlr_ref, wa2_ref, ba_ref, g_ref):
    a = jnp.dot(u_ref[...], wlr_ref[...], preferred_element_type=F32)
    x = jnp.dot(a.astype(BF16), wa2_ref[...], preferred_element_type=F32) + ba_ref[...]
    g_ref[...] = (jnp.minimum(x, 0.0) - jnp.log1p(jnp.exp(-jnp.abs(x)))) * (1.0 / GLA_TAU)


def _decay(u, w_lr, w_alpha2, b_alpha):
    n, d = u.shape
    rank, c = w_alpha2.shape
    assert rank <= LANES
    wlr = jnp.zeros((d, LANES), BF16).at[:, :rank].set(w_lr.astype(BF16))
    wa2 = jnp.zeros((LANES, c), BF16).at[:rank].set(w_alpha2.astype(BF16))
    tr = _tile(n, 2 * ROW_TILE, BF16_SUBLANES)
    return pl.pallas_call(
        _decay_body,
        grid=(n // tr,),
        in_specs=[
            pl.BlockSpec((tr, d), lambda i: (i, 0)),
            pl.BlockSpec((d, LANES), lambda i: (0, 0)),
            pl.BlockSpec((LANES, c), lambda i: (0, 0)),
            pl.BlockSpec((1, c), lambda i: (0, 0)),
        ],
        out_specs=pl.BlockSpec((tr, c), lambda i: (i, 0)),
        out_shape=jax.ShapeDtypeStruct((n, c), F32),
        compiler_params=_params(("parallel",)),
        name="gla_decay",
    )(u, wlr, wa2, b_alpha.reshape(1, c).astype(F32))


def _rope_tables(pos, dk):
    rot = dk // 4
    half = rot // 2
    inv_freq = jnp.power(jnp.float32(ROPE_THETA), -jnp.arange(half, dtype=F32) * (2.0 / rot))
    ang = pos.astype(F32)[:, None] * inv_freq[None, :]
    cos, sin = jnp.cos(ang), jnp.sin(ang)
    n = pos.shape[0]
    pad = jnp.zeros((n, dk - rot), F32)
    zero = jnp.zeros((n, half), F32)
    c = jnp.concatenate([cos, cos, pad + 1.0], axis=1)
    s1 = jnp.concatenate([-sin, zero, pad], axis=1)
    s2 = jnp.concatenate([zero, sin, pad], axis=1)
    return c, s1, s2


def _prep_body(zq_ref, zk_ref, zv_ref, c_ref, s1_ref, s2_ref, q_o, kf_o, kb_o, vf_o, vb_o, *, n_maps, dk):
    half = dk // 8
    c, s1, s2 = c_ref[...], s1_ref[...], s2_ref[...]
    q_scale = (dk ** -0.5) * math.log2(math.e)

    def rope(x):
        return x * c + pltpu.roll(x, dk - half, 1) * s1 + pltpu.roll(x, half, 1) * s2

    for m in range(n_maps):
        sl = slice(m * dk, (m + 1) * dk)
        q_o[:, sl] = (rope(zq_ref[:, sl]) * q_scale).astype(q_o.dtype)
        k = rope(zk_ref[:, sl])
        kf_o[:, sl] = k
        kb_o[:, sl] = k.astype(kb_o.dtype)
    v = zv_ref[...]
    vf_o[...] = v
    vb_o[...] = v.astype(vb_o.dtype)


def _prep(z, row0, n_rows, pos, n_maps, dk, w_v, offs):
    w_qk = n_maps * dk
    tr = _tile(n_rows, ROW_TILE, BF16_SUBLANES)
    assert row0 % tr == 0 and offs[0] % w_qk == 0 and offs[1] % w_qk == 0 and offs[2] % w_v == 0
    rb = row0 // tr
    c, s1, s2 = _rope_tables(pos, dk)
    zspec = lambda w, off: pl.BlockSpec((tr, w), functools.partial(lambda i, o: (i + rb, o), o=off // w))
    tspec = pl.BlockSpec((tr, dk), lambda i: (i, 0))
    ospec = lambda w: pl.BlockSpec((tr, w), lambda i: (i, 0))
    sds = lambda w, dt: jax.ShapeDtypeStruct((n_rows, w), dt)
    return pl.pallas_call(
        functools.partial(_prep_body, n_maps=n_maps, dk=dk),
        grid=(n_rows // tr,),
        in_specs=[zspec(w_qk, offs[0]), zspec(w_qk, offs[1]), zspec(w_v, offs[2]), tspec, tspec, tspec],
        out_specs=[ospec(w_qk), ospec(w_qk), ospec(w_qk), ospec(w_v), ospec(w_v)],
        out_shape=[sds(w_qk, BF16), sds(w_qk, F32), sds(w_qk, BF16), sds(w_v, F32), sds(w_v, BF16)],
        compiler_params=_params(("parallel",), VMEM_LIMIT_BYTES),
        name="qkv_rope",
    )(z, z, z, c, s1, s2)


def _lambda_value(lq1, lk1, lq2, lk2):
    a = jnp.exp(jnp.sum(lq1[...] * lk1[...], axis=-1, keepdims=True))
    b = jnp.exp(jnp.sum(lq2[...] * lk2[...], axis=-1, keepdims=True))
    return a - b + LAMBDA_INIT


def _diff_finish(a1, l1, a2, l2, lam, sw):
    o = a1 / l1 - lam * (a2 / l2)
    y = o * lax.rsqrt(jnp.mean(o * o, axis=-1, keepdims=True) + NORM_EPS)
    return (y * sw) * (1.0 - LAMBDA_INIT)


def _online_softmax_update(s, m_prev, l_prev):
    m_new = jnp.maximum(m_prev, jnp.max(s, axis=-1, keepdims=True))
    alpha = jnp.exp2(m_prev - m_new)
    p = jnp.exp2(s - m_new)
    return m_new, alpha, p, alpha * l_prev + jnp.sum(p, axis=-1, keepdims=True)


def _attn_prompt_body(qi_ref, kj_ref, q_ref, k_ref, v_ref, lq1, lk1, lq2, lk2, sw_ref, o_ref,
                      m_sc, l_sc, acc_sc, *, dk):
    t = pl.program_id(1)
    i = qi_ref[t]
    j = kj_ref[t]

    @pl.when(j == 0)
    def _():
        m_sc[...] = jnp.full(m_sc.shape, -jnp.inf, F32)
        l_sc[...] = jnp.zeros(l_sc.shape, F32)
        acc_sc[...] = jnp.zeros(acc_sc.shape, F32)

    t_blk, dv = acc_sc.shape[1], acc_sc.shape[2]
    rb = min(t_blk, ATTN_ROW_BLOCK)

    def lanes(x, width):
        return x if width == LANES else jnp.concatenate([x] * (width // LANES), axis=1)

    def step(masked):
        for c in range(2):
            for r in range(t_blk // rb):
                rows = slice(r * rb, (r + 1) * rb)
                n_k = (r + 1) * rb if masked else t_blk
                s = _nt_dot(q_ref[rows, c * dk:(c + 1) * dk], k_ref[0:n_k, c * dk:(c + 1) * dk])
                if masked:
                    row = lax.broadcasted_iota(jnp.int32, s.shape, 0) + r * rb
                    col = lax.broadcasted_iota(jnp.int32, s.shape, 1)
                    s = jnp.where(col <= row, s, -jnp.inf)
                m_prev = m_sc[c, rows, :]
                m_new = jnp.maximum(m_prev, jnp.max(s, axis=-1, keepdims=True))
                alpha = jnp.exp2(m_prev - m_new)
                p = jnp.exp2(s - lanes(m_new, n_k))
                l_sc[c, rows, :] = alpha * l_sc[c, rows, :] + jnp.sum(p, axis=-1, keepdims=True)
                acc_sc[c, rows, :] = lanes(alpha, dv) * acc_sc[c, rows, :] + jnp.dot(
                    p.astype(BF16), v_ref[0:n_k, :], preferred_element_type=F32)
                m_sc[c, rows, :] = m_new

    @pl.when(j < i)
    def _():
        step(False)

    @pl.when(j == i)
    def _():
        step(True)
        lam = _lambda_value(lq1, lk1, lq2, lk2)
        o_ref[...] = _diff_finish(acc_sc[0], l_sc[0][:, 0:1], acc_sc[1], l_sc[1][:, 0:1], lam,
                                  sw_ref[...]).astype(o_ref.dtype)


def _attn_prompt(q, k, v, lam_rows, subln_w, n_heads, dk, dv):
    s_len = q.shape[0]
    t_blk = _tile(s_len, ATTN_TILE, LANES)
    nb = s_len // t_blk
    qi = np.concatenate([np.full(i + 1, i, np.int32) for i in range(nb)])
    kj = np.concatenate([np.arange(i + 1, dtype=np.int32) for i in range(nb)])
    rowspec = pl.BlockSpec((1, dk), lambda h, t, qi, kj: (0, 0))
    grid_spec = pltpu.PrefetchScalarGridSpec(
        num_scalar_prefetch=2,
        grid=(n_heads, len(qi)),
        in_specs=[
            pl.BlockSpec((t_blk, 2 * dk), lambda h, t, qi, kj: (qi[t], h)),
            pl.BlockSpec((t_blk, 2 * dk), lambda h, t, qi, kj: (kj[t], h)),
            pl.BlockSpec((t_blk, dv), lambda h, t, qi, kj: (kj[t], h)),
            rowspec, rowspec, rowspec, rowspec,
            pl.BlockSpec((1, dv), lambda h, t, qi, kj: (0, 0)),
        ],
        out_specs=pl.BlockSpec((t_blk, dv), lambda h, t, qi, kj: (qi[t], h)),
        scratch_shapes=[
            pltpu.VMEM((2, t_blk, LANES), F32),
            pltpu.VMEM((2, t_blk, LANES), F32),
            pltpu.VMEM((2, t_blk, dv), F32),
        ],
    )
    return pl.pallas_call(
        functools.partial(_attn_prompt_body, dk=dk),
        grid_spec=grid_spec,
        out_shape=jax.ShapeDtypeStruct((s_len, n_heads * dv), BF16),
        compiler_params=_params(("parallel", "arbitrary"), VMEM_LIMIT_BYTES),
        name="diff_attn_prompt",
    )(jnp.asarray(qi), jnp.asarray(kj), q, k, v, *lam_rows, subln_w)


def _attn_sample_body(pt_ref, q_ref, *refs, n_pg, n_steps, n_heads, dv, lp, l_new):
    n_vc = dv // LANES
    k_refs, v_refs = refs[:n_pg], refs[n_pg:n_pg * (1 + n_vc)]
    kn_ref, vn_ref, lq1, lk1, lq2, lk2, sw_ref, o_ref, m_sc, l_sc, acc_sc = refs[n_pg * (1 + n_vc):]
    step = pl.program_id(1)

    @pl.when(step == 0)
    def _():
        m_sc[...] = jnp.full(m_sc.shape, -jnp.inf, F32)
        l_sc[...] = jnp.zeros(l_sc.shape, F32)
        acc_sc[...] = jnp.zeros(acc_sc.shape, F32)

    q = q_ref[...]

    def update(kb, vb, mask):
        s = _nt_dot(q, kb)
        if mask is not None:
            s = jnp.where(mask, s, -jnp.inf)
        m_new, alpha, p, l_next = _online_softmax_update(s, m_sc[...], l_sc[...])
        for h in range(n_heads):
            rows = slice(2 * h * lp, 2 * (h + 1) * lp)
            cols = slice(h * dv, (h + 1) * dv)
            acc_sc[rows, cols] = alpha[rows] * acc_sc[rows, cols] + jnp.dot(
                p[rows].astype(BF16), vb[:, cols], preferred_element_type=F32)
        m_sc[...] = m_new
        l_sc[...] = l_next

    @pl.when(step < n_steps)
    def _():
        def page_rows(refs_, groups):
            rows = refs_[0].shape[0] // groups
            return jnp.concatenate(
                [ref[pl.ds(gi, rows, stride=groups), :].astype(BF16) for gi in range(groups) for ref in refs_],
                axis=1)

        kb = jnp.concatenate([page_rows(k_refs[r:r + 1], 2 * n_heads) for r in range(n_pg)], axis=0)
        vb = jnp.concatenate([page_rows(v_refs[r * n_vc:(r + 1) * n_vc], n_heads) for r in range(n_pg)], axis=0)
        update(kb, vb, None)

    @pl.when(step == n_steps)
    def _():
        shape = (q.shape[0], kn_ref.shape[0])
        row_i = jnp.bitwise_and(lax.broadcasted_iota(jnp.int32, shape, 0), lp - 1)
        col = lax.broadcasted_iota(jnp.int32, shape, 1)
        update(kn_ref[...], vn_ref[...], jnp.logical_and(col < l_new, col <= row_i))
        lam = _lambda_value(lq1, lk1, lq2, lk2)
        for h in range(n_heads):
            r0 = 2 * h * lp
            cols = slice(h * dv, (h + 1) * dv)
            o_ref[:, cols] = _diff_finish(
                acc_sc[r0:r0 + lp, cols], l_sc[r0:r0 + lp], acc_sc[r0 + lp:r0 + 2 * lp, cols],
                l_sc[r0 + lp:r0 + 2 * lp], lam, sw_ref[...])


def _attn_sample(q_rot, k_new, v_new, cache_k, cache_v, page_table, lam_rows, subln_w, n_heads, dk, dv):
    b, l_new, w_qk = q_rot.shape
    n_maps = 2 * n_heads
    n_pages = page_table.shape[1]
    page = cache_v.shape[1] // n_heads
    w_v = n_heads * dv
    lp = max(SUBLANES, 1 << (l_new - 1).bit_length())
    assert l_new <= page
    qp = jnp.pad(q_rot.reshape(b, l_new, n_maps, dk), ((0, 0), (0, lp - l_new), (0, 0), (0, 0)))
    eye = jnp.eye(n_maps, dtype=BF16)
    q_bd = (qp.transpose(0, 2, 1, 3)[:, :, :, None, :] * eye[None, :, None, :, None]).reshape(b, n_maps * lp, w_qk)
    kn = jnp.pad(k_new, ((0, 0), (0, page - l_new), (0, 0)))
    vn = jnp.pad(v_new, ((0, 0), (0, page - l_new), (0, 0)))
    n_pg = _tile(n_pages, PAGES_PER_STEP, 1)
    n_steps = n_pages // n_pg

    assert dk == LANES and dv % LANES == 0
    n_vc = dv // LANES

    def page_map(r, cb=0):
        return lambda bi, s, pt: (pt[bi * n_pages + jnp.minimum(s, n_steps - 1) * n_pg + r], 0, cb)

    rowspec = pl.BlockSpec((1, dk), lambda bi, s, pt: (0, 0))
    seqspec = lambda rows, w: pl.BlockSpec((None, rows, w), lambda bi, s, pt: (bi, 0, 0))
    grid_spec = pltpu.PrefetchScalarGridSpec(
        num_scalar_prefetch=1,
        grid=(b, n_steps + 1),
        in_specs=(
            [seqspec(n_maps * lp, w_qk)]
            + [pl.BlockSpec((None, page * n_maps, dk), page_map(r)) for r in range(n_pg)]
            + [pl.BlockSpec((None, page * n_heads, LANES), page_map(r, cb))
               for r in range(n_pg) for cb in range(n_vc)]
            + [seqspec(page, w_qk), seqspec(page, w_v), rowspec, rowspec, rowspec, rowspec,
               pl.BlockSpec((1, dv), lambda bi, s, pt: (0, 0))]
        ),
        out_specs=seqspec(lp, w_v),
        scratch_shapes=[
            pltpu.VMEM((n_maps * lp, 1), F32),
            pltpu.VMEM((n_maps * lp, 1), F32),
            pltpu.VMEM((n_maps * lp, w_v), F32),
        ],
    )
    out = pl.pallas_call(
        functools.partial(_attn_sample_body, n_pg=n_pg, n_steps=n_steps, n_heads=n_heads, dv=dv, lp=lp,
                          l_new=l_new),
        grid_spec=grid_spec,
        out_shape=jax.ShapeDtypeStruct((b, lp, w_v), F32),
        compiler_params=_params(("parallel", "arbitrary"), VMEM_LIMIT_BYTES),
        name="diff_attn_sample",
    )(page_table.reshape(-1), q_bd, *([cache_k] * n_pg), *([cache_v] * (n_pg * n_vc)), kn, vn, *lam_rows, subln_w)
    return out[:, :l_new]


def _gla_constants(c):
    levels = c.bit_length() - 1
    assert 1 << levels == c
    idx = np.arange(c)
    i, t = idx[:, None], idx[None, :]
    sums = [t <= i, t > i]
    masks = [i == t]
    for lvl in range(1, levels + 1):
        blk, half = 1 << lvl, 1 << (lvl - 1)
        pos = idx % blk
        mid = (idx // blk) * blk + half - 1
        right, left = pos >= half, pos < half
        sums.append(right[:, None] & (t > mid[:, None]) & (t <= i))
        sums.append(left[:, None] & (t > i) & (t <= mid[:, None]))
        masks.append((idx[:, None] // blk == idx[None, :] // blk) & right[:, None] & left[None, :])
    return (jnp.asarray(np.concatenate(sums, 0).astype(np.float32), BF16),
            jnp.asarray(np.stack(masks, 0).astype(np.float32), F32))


def _gla_body(q_ref, k_ref, v_ref, r_ref, g_ref, s0_ref, sum_ref, mask_ref, gw_ref, o_ref, sout_ref, s_sc,
              *, c, levels, dk):
    ci = pl.program_id(2)

    @pl.when(ci == 0)
    def _():
        s_sc[...] = s0_ref[...].astype(F32)

    g = g_ref[...]
    g1 = g.astype(BF16)
    rem = g - g1.astype(F32)
    g2 = rem.astype(BF16)
    g3 = (rem - g2.astype(F32)).astype(BF16)
    sums = sum_ref[...]
    seg = (jnp.dot(sums, g1, preferred_element_type=F32) + jnp.dot(sums, g2, preferred_element_type=F32)
           + jnp.dot(sums, g3, preferred_element_type=F32))
    ones = jnp.ones((c, LANES), BF16)
    b_last = (_tn_dot(g1, ones) + _tn_dot(g2, ones) + _tn_dot(g3, ones))[:, 0:1]

    q = q_ref[...] * (dk ** -0.5)
    k = k_ref[...]
    v = v_ref[...].astype(BF16)
    state = s_sc[...]
    o = jnp.dot((q * jnp.exp(seg[0:c])).astype(BF16), state.astype(BF16), preferred_element_type=F32)
    a = mask_ref[0] * _nt_dot(q.astype(BF16), k.astype(BF16))
    for lvl in range(1, levels + 1):
        qd = (q * jnp.exp(seg[2 * lvl * c:(2 * lvl + 1) * c])).astype(BF16)
        kd = (k * jnp.exp(seg[(2 * lvl + 1) * c:(2 * lvl + 2) * c])).astype(BF16)
        a = a + mask_ref[lvl] * _nt_dot(qd, kd)
    o = o + jnp.dot(a.astype(BF16), v, preferred_element_type=F32)
    k_tail = (k * jnp.exp(seg[c:2 * c])).astype(BF16)
    new_state = state * jnp.exp(b_last) + _tn_dot(k_tail, v)
    s_sc[...] = new_state

    y = o * lax.rsqrt(jnp.mean(o * o, axis=-1, keepdims=True) + NORM_EPS) * gw_ref[...]
    r = r_ref[...]
    o_ref[...] = (y * (r * _sigmoid(r))).astype(o_ref.dtype)

    @pl.when(ci == pl.num_programs(2) - 1)
    def _():
        sout_ref[...] = new_state


def _gla(q, k, v, r, g, offs, s0, gla_norm_w, c):
    b, n_heads, dk, dv = s0.shape
    rows = g.shape[1]
    n_chunks = rows // c
    levels = c.bit_length() - 1
    sums, masks = _gla_constants(c)
    assert all(off % w == 0 for off, w in zip(offs, (dk, dk, dv, dv)))
    col = lambda w, off: pl.BlockSpec(
        (None, c, w), functools.partial(lambda bi, h, ci, o: (bi, ci, h + o), o=off // w))
    const3 = lambda shp: pl.BlockSpec(shp, lambda bi, h, ci: (0,) * len(shp))
    sspec = pl.BlockSpec((None, None, dk, dv), lambda bi, h, ci: (bi, h, 0, 0))
    return pl.pallas_call(
        functools.partial(_gla_body, c=c, levels=levels, dk=dk),
        grid=(b, n_heads, n_chunks),
        in_specs=[col(dk, offs[0]), col(dk, offs[1]), col(dv, offs[2]), col(dv, offs[3]), col(dk, 0), sspec,
                  const3(sums.shape), const3(masks.shape), const3((1, dv))],
        out_specs=[col(dv, 0), sspec],
        out_shape=[jax.ShapeDtypeStruct((b, rows, n_heads * dv), BF16),
                   jax.ShapeDtypeStruct((b, n_heads, dk, dv), F32)],
        scratch_shapes=[pltpu.VMEM((dk, dv), F32)],
        compiler_params=_params(("parallel", "parallel", "arbitrary"), VMEM_LIMIT_BYTES),
        name="gla",
    )(q, k, v, r, g, s0, sums, masks, gla_norm_w.reshape(1, dv).astype(F32))


def _router_body(x_ref, w_ref, wr_ref, br_ref, h_ref, idx_ref, gate_ref):
    x = x_ref[...]
    h = (x * lax.rsqrt(jnp.mean(x * x, axis=-1, keepdims=True) + NORM_EPS)) * w_ref[...]
    h_ref[...] = h
    logits = jnp.dot(h, wr_ref[...], precision=lax.Precision.HIGHEST, preferred_element_type=F32) + br_ref[...]
    lane = lax.broadcasted_iota(jnp.int32, logits.shape, 1)
    idx_out = jnp.zeros(logits.shape, jnp.int32)
    val_out = jnp.zeros(logits.shape, F32)
    top0 = None
    for kk in range(TOP_K):
        mx = jnp.max(logits, axis=-1, keepdims=True)
        idx = jnp.min(jnp.where(logits == mx, lane, LANES), axis=-1, keepdims=True)
        top0 = mx if top0 is None else top0
        idx_out = jnp.where(lane == kk, idx, idx_out)
        val_out = jnp.where(lane == kk, jnp.exp(mx - top0), val_out)
        logits = jnp.where(lane == idx, -jnp.inf, logits)
    idx_ref[...] = idx_out
    gate_ref[...] = val_out / jnp.sum(val_out, axis=-1, keepdims=True)


def _router(x, norm_w, w_router, b_router):
    n, d = x.shape
    e = w_router.shape[1]
    assert TOP_K <= e <= LANES
    wr = jnp.zeros((d, LANES), F32).at[:, :e].set(w_router.astype(F32))
    br = jnp.full((1, LANES), -jnp.inf, F32).at[0, :e].set(b_router.astype(F32))
    tr = _tile(n, ROW_TILE, SUBLANES)
    row = lambda w: pl.BlockSpec((tr, w), lambda i: (i, 0))
    fixed = lambda r, w: pl.BlockSpec((r, w), lambda i: (0, 0))
    return pl.pallas_call(
        _router_body,
        grid=(n // tr,),
        in_specs=[row(d), fixed(1, d), fixed(d, LANES), fixed(1, LANES)],
        out_specs=[row(d), row(LANES), row(LANES)],
        out_shape=[jax.ShapeDtypeStruct((n, d), F32), jax.ShapeDtypeStruct((n, LANES), jnp.int32),
                   jax.ShapeDtypeStruct((n, LANES), F32)],
        compiler_params=_params(("parallel",), VMEM_LIMIT_BYTES),
        name="ffn_norm_router",
    )(x, norm_w.reshape(1, d).astype(F32), wr, br)


def _row_gather(src_hbm, idx_ref, buf, sem, slot, n):
    def copy(src_row, r):
        return pltpu.make_async_copy(src_hbm.at[pl.ds(src_row, 1), :], buf.at[slot, pl.ds(r, 1), :], sem.at[slot])

    def issue_one(r, carry):
        copy(idx_ref[0, r], r).start()
        return carry

    def wait_one(r, carry):
        copy(0, r).wait()
        return carry

    def issue():
        lax.fori_loop(0, n, issue_one, 0, unroll=8)

    def wait():
        lax.fori_loop(0, n, wait_one, 0, unroll=8)

    return issue, wait


def _gather_body(valid_ref, tok_ref, tok_next_ref, h_hbm, o_ref, buf, sem, *, rows):
    t = pl.program_id(0)
    slot = t % 2
    issue_cur, wait_cur = _row_gather(h_hbm, tok_ref, buf, sem, slot, rows)
    issue_next, _ = _row_gather(h_hbm, tok_next_ref, buf, sem, 1 - slot, rows)

    @pl.when(jnp.logical_and(t == 0, valid_ref[0] > 0))
    def _():
        issue_cur()

    last = pl.num_programs(0) - 1

    @pl.when(jnp.logical_and(t < last, valid_ref[jnp.minimum(t + 1, last)] > 0))
    def _():
        issue_next()

    @pl.when(valid_ref[t] > 0)
    def _():
        wait_cur()
        o_ref[...] = buf[slot].astype(o_ref.dtype)

    @pl.when(valid_ref[t] == 0)
    def _():
        o_ref[...] = jnp.zeros(o_ref.shape, o_ref.dtype)


def _gather_rows(h, slot_tok, tile_valid, rows):
    n_slots = slot_tok.shape[0]
    d = h.shape[1]
    n_tiles = n_slots // rows
    grid_spec = pltpu.PrefetchScalarGridSpec(
        num_scalar_prefetch=1,
        grid=(n_tiles,),
        in_specs=[
            pl.BlockSpec((None, 1, rows), lambda t, valid: (t, 0, 0), memory_space=pltpu.SMEM),
            pl.BlockSpec((None, 1, rows), lambda t, valid: (jnp.minimum(t + 1, n_tiles - 1), 0, 0),
                         memory_space=pltpu.SMEM),
            pl.BlockSpec(memory_space=pl.ANY),
        ],
        out_specs=pl.BlockSpec((rows, d), lambda t, valid: (t, 0)),
        scratch_shapes=[pltpu.VMEM((2, rows, d), F32), pltpu.SemaphoreType.DMA((2,))],
    )
    return pl.pallas_call(
        functools.partial(_gather_body, rows=rows),
        grid_spec=grid_spec,
        out_shape=jax.ShapeDtypeStruct((n_slots, d), BF16),
        compiler_params=_params(("arbitrary",), VMEM_LIMIT_BYTES),
        name="moe_gather",
    )(tile_valid, slot_tok.reshape(n_tiles, 1, rows), slot_tok.reshape(n_tiles, 1, rows), h)


def _for_row_tiles(n_rows, o_ref, tm, tile_fn):
    n_small = (n_rows + tm - 1) // tm
    n_big = n_small // 2

    def big_tile(r, carry):
        tile_fn(pl.multiple_of(r * (2 * tm), 2 * tm), 2 * tm)
        return carry

    def zero(r, carry):
        o_ref[pl.ds(pl.multiple_of(r * tm, tm), tm), :] = jnp.zeros((tm, o_ref.shape[1]), o_ref.dtype)
        return carry

    lax.fori_loop(0, n_big, big_tile, 0)

    @pl.when(n_small % 2 == 1)
    def _():
        tile_fn(pl.multiple_of(n_big * (2 * tm), tm), tm)

    lax.fori_loop(n_small, o_ref.shape[0] // tm, zero, 0)


def _expert_up_body(ge_ref, gr_ref, nv_ref, x_ref, wg_ref, wu_ref, bg_ref, bu_ref, o_ref, w_sc, *, tm):
    g = pl.program_id(0)
    tn = o_ref.shape[1]

    @pl.when(g < nv_ref[0])
    def _():
        w_sc[:, :tn] = wg_ref[...].astype(BF16)
        w_sc[:, tn:] = wu_ref[...].astype(BF16)
        bias = jnp.concatenate([bg_ref[...], bu_ref[...]], axis=1)

        def tile(start, size):
            rs = pl.ds(start, size)
            gu = jnp.dot(x_ref[rs, :], w_sc[...], preferred_element_type=F32) + bias
            gt = jnp.minimum(gu[:, :tn], SWIGLU_LIMIT)
            up = jnp.clip(gu[:, tn:], -SWIGLU_LIMIT, SWIGLU_LIMIT)
            o_ref[rs, :] = ((up + 1.0) * (gt * _sigmoid(SWIGLU_ALPHA * gt))).astype(o_ref.dtype)

        _for_row_tiles(gr_ref[g], o_ref, tm, tile)


def _expert_down_body(ge_ref, gr_ref, nv_ref, x_ref, w_ref, b_ref, o_ref, w_sc, *, tm):
    g = pl.program_id(0)

    @pl.when(g < nv_ref[0])
    def _():
        w_sc[...] = w_ref[...].astype(BF16)

        def tile(start, size):
            rs = pl.ds(start, size)
            o_ref[rs, :] = jnp.dot(x_ref[rs, :], w_sc[...], preferred_element_type=F32) + b_ref[...]

        _for_row_tiles(gr_ref[g], o_ref, tm, tile)


def _expert_ffn(xs, group_expert, group_rows, n_valid, w_gate_up, b_gate_up, w_down, b_down, tm):
    n_groups, g_rows, d = xs.shape
    e, _, two_de = w_gate_up.shape
    de = two_de // 2
    tn_u = _tile(de, MOE_TN, LANES)
    tn_d = _tile(d, MOE_TN_DOWN, LANES)
    nj_u, nj_d = de // tn_u, d // tn_d

    def maps(nj):
        gv = lambda g, nv: jnp.minimum(g, nv[0] - 1)
        jv = lambda g, j, nv: jnp.where(g < nv[0], j, nj - 1)
        return gv, jv

    gv, jv = maps(nj_u)
    act = pl.pallas_call(
        functools.partial(_expert_up_body, tm=tm),
        grid_spec=pltpu.PrefetchScalarGridSpec(
            num_scalar_prefetch=3,
            grid=(n_groups, nj_u),
            in_specs=[
                pl.BlockSpec((None, g_rows, d), lambda g, j, ge, gr, nv: (gv(g, nv), 0, 0)),
                pl.BlockSpec((None, d, tn_u), lambda g, j, ge, gr, nv: (ge[gv(g, nv)], 0, jv(g, j, nv))),
                pl.BlockSpec((None, d, tn_u), lambda g, j, ge, gr, nv: (ge[gv(g, nv)], 0, nj_u + jv(g, j, nv))),
                pl.BlockSpec((None, 1, tn_u), lambda g, j, ge, gr, nv: (ge[gv(g, nv)], 0, jv(g, j, nv))),
                pl.BlockSpec((None, 1, tn_u), lambda g, j, ge, gr, nv: (ge[gv(g, nv)], 0, nj_u + jv(g, j, nv))),
            ],
            out_specs=pl.BlockSpec((None, g_rows, tn_u), lambda g, j, ge, gr, nv: (gv(g, nv), 0, jv(g, j, nv))),
            scratch_shapes=[pltpu.VMEM((d, 2 * tn_u), BF16)],
        ),
        out_shape=jax.ShapeDtypeStruct((n_groups, g_rows, de), BF16),
        compiler_params=_params(("arbitrary", "arbitrary"), VMEM_LIMIT_BYTES),
        name="moe_gate_up",
    )(group_expert, group_rows, n_valid, xs, w_gate_up, w_gate_up, b_gate_up.reshape(e, 1, two_de),
      b_gate_up.reshape(e, 1, two_de))

    gv, jv = maps(nj_d)
    return pl.pallas_call(
        functools.partial(_expert_down_body, tm=tm),
        grid_spec=pltpu.PrefetchScalarGridSpec(
            num_scalar_prefetch=3,
            grid=(n_groups, nj_d),
            in_specs=[
                pl.BlockSpec((None, g_rows, de), lambda g, j, ge, gr, nv: (gv(g, nv), 0, 0)),
                pl.BlockSpec((None, de, tn_d), lambda g, j, ge, gr, nv: (ge[gv(g, nv)], 0, jv(g, j, nv))),
                pl.BlockSpec((None, 1, tn_d), lambda g, j, ge, gr, nv: (ge[gv(g, nv)], 0, jv(g, j, nv))),
            ],
            out_specs=pl.BlockSpec((None, g_rows, tn_d), lambda g, j, ge, gr, nv: (gv(g, nv), 0, jv(g, j, nv))),
            scratch_shapes=[pltpu.VMEM((de, tn_d), BF16)],
        ),
        out_shape=jax.ShapeDtypeStruct((n_groups, g_rows, d), F32),
        compiler_params=_params(("arbitrary", "arbitrary"), VMEM_LIMIT_BYTES),
        name="moe_down",
    )(group_expert, group_rows, n_valid, act, w_down, b_down.reshape(e, 1, d))


def _combine_body(pos_ref, pos_next_ref, y_hbm, gate_ref, x_ref, w_ref, o_ref, buf, sem, *, rows):
    t = pl.program_id(0)
    slot = t % 2
    issue_cur, wait_cur = _row_gather(y_hbm, pos_ref, buf, sem, slot, TOP_K * rows)
    issue_next, _ = _row_gather(y_hbm, pos_next_ref, buf, sem, 1 - slot, TOP_K * rows)

    @pl.when(t == 0)
    def _():
        issue_cur()

    @pl.when(t < pl.num_programs(0) - 1)
    def _():
        issue_next()

    wait_cur()
    gates = gate_ref[...]
    x = x_ref[...]
    for kk in range(TOP_K):
        x = x + gates[:, kk:kk + 1] * buf[slot, kk * rows:(kk + 1) * rows, :]
    y = x * lax.rsqrt(jnp.mean(x * x, axis=-1, keepdims=True) + NORM_EPS)
    o_ref[...] = y * w_ref[...]


def _combine(pos_tiles, y_slots, gates, x1, norm_w, row0, n_rows, rows):
    d = x1.shape[1]
    assert row0 % rows == 0 and n_rows % rows == 0
    rb = row0 // rows
    n_tiles = n_rows // rows
    grid_spec = pltpu.PrefetchScalarGridSpec(
        num_scalar_prefetch=0,
        grid=(n_tiles,),
        in_specs=[
            pl.BlockSpec((None, 1, TOP_K * rows), lambda t: (t + rb, 0, 0), memory_space=pltpu.SMEM),
            pl.BlockSpec((None, 1, TOP_K * rows), lambda t: (jnp.minimum(t + 1, n_tiles - 1) + rb, 0, 0),
                         memory_space=pltpu.SMEM),
            pl.BlockSpec(memory_space=pl.ANY),
            pl.BlockSpec((rows, LANES), lambda t: (t + rb, 0)),
            pl.BlockSpec((rows, d), lambda t: (t + rb, 0)),
            pl.BlockSpec((1, d), lambda t: (0, 0)),
        ],
        out_specs=pl.BlockSpec((rows, d), lambda t: (t, 0)),
        scratch_shapes=[pltpu.VMEM((2, TOP_K * rows, d), F32), pltpu.SemaphoreType.DMA((2,))],
    )
    return pl.pallas_call(
        functools.partial(_combine_body, rows=rows),
        grid_spec=grid_spec,
        out_shape=jax.ShapeDtypeStruct((n_rows, d), F32),
        compiler_params=_params(("arbitrary",), VMEM_LIMIT_BYTES),
        name="moe_combine_norm",
    )(pos_tiles, pos_tiles, y_slots, gates, x1, norm_w.reshape(1, d).astype(F32))


def _routing_tables(expert_ids, n_experts, g_rows, n_groups, gather_rows):
    n = expert_ids.shape[0]
    m = n * TOP_K
    e_flat = expert_ids.reshape(-1)
    onehot = (e_flat[:, None] == jnp.arange(n_experts, dtype=jnp.int32)[None, :]).astype(jnp.int32)
    before = jnp.cumsum(onehot, axis=0) - onehot
    rank = jnp.take_along_axis(before, e_flat[:, None], axis=1)[:, 0]
    counts = jnp.sum(onehot, axis=0)
    groups_per_e = (counts + g_rows - 1) // g_rows
    g_end = jnp.cumsum(groups_per_e)
    g_base = g_end - groups_per_e
    slot = (g_base[e_flat] + rank // g_rows) * g_rows + rank % g_rows
    n_valid = g_end[-1]
    gid = jnp.arange(n_groups, dtype=jnp.int32)
    g_exp = jnp.minimum(jnp.searchsorted(g_end, gid, side="right"), n_experts - 1).astype(jnp.int32)
    g_cnt = jnp.clip(counts[g_exp] - (gid - g_base[g_exp]) * g_rows, 0, g_rows)
    g_cnt = jnp.where(gid < n_valid, g_cnt, 0).astype(jnp.int32)
    slot_tok = jnp.zeros((n_groups * g_rows,), jnp.int32).at[slot].set(jnp.arange(m, dtype=jnp.int32) // TOP_K)
    tiles_per_g = g_rows // gather_rows
    tile_start = (jnp.arange(n_groups * tiles_per_g, dtype=jnp.int32) % tiles_per_g) * gather_rows
    tile_valid = (tile_start < jnp.repeat(g_cnt, tiles_per_g)).astype(jnp.int32)
    return slot.astype(jnp.int32), slot_tok, tile_valid, g_exp, g_cnt, n_valid.astype(jnp.int32).reshape(1)


def kernel(x_prompt, x_sample, cache_k, cache_v, state_gla, page_table, norm_mix_w, w_in, lambda_q1, lambda_k1, lambda_q2, lambda_k2, subln_w, w_alpha2, b_alpha, gla_norm_w, w_branch_a, w_branch_b, w_out, norm_ffn_w, w_router, b_router, w_gate_up, b_gate_up, w_down, b_down, norm_final_w):
    bp, s_len, d = x_prompt.shape
    bs, l_new, _ = x_sample.shape
    depth, n_pool, page, n_maps, dk_a = cache_k.shape
    _, _, _, h_a, dv_a = cache_v.shape
    _, _, h_b, dk_b, dv_b = state_gla.shape
    rank = w_alpha2.shape[1]
    n_experts = w_router.shape[2]
    assert depth == 1 and bp == 1 and n_maps == 2 * h_a
    n_p, n_s = bp * s_len, bs * l_new
    n_all = n_p + n_s
    sizes = (n_maps * dk_a, n_maps * dk_a, h_a * dv_a, h_b * dk_b, h_b * dk_b, h_b * dv_b, h_b * dv_b, d, d, rank)
    offs = [int(o) for o in np.cumsum((0,) + sizes)]
    n_main = offs[9]
    assert w_in.shape[2] == offs[10]
    past = page_table.shape[1] * page

    x_all = jnp.concatenate([x_prompt.reshape(n_p, d), x_sample.reshape(n_s, d)], axis=0)
    u = _rmsnorm(x_all, norm_mix_w[0], BF16)
    w_in_b = w_in[0].astype(BF16)
    z = _matmul([(u, w_in_b, 0)], [], n_main, _epi_plain, F32, "in_proj")
    g_all = _decay(u, w_in_b[:, n_main:], w_alpha2[0], b_alpha[0])

    lam_rows = [p[0].reshape(1, dk_a).astype(F32) for p in (lambda_q1, lambda_k1, lambda_q2, lambda_k2)]
    sw = subln_w[0].reshape(1, dv_a).astype(F32)

    pos_p = jnp.arange(s_len, dtype=jnp.int32)
    q_p, kf_p, kb_p, vf_p, vb_p = _prep(z, 0, n_p, pos_p, n_maps, dk_a, h_a * dv_a, offs[0:3])
    oa_p = _attn_prompt(q_p, kb_p, vb_p, lam_rows, sw, h_a, dk_a, dv_a)
    z3 = z.reshape(1, n_all, n_main)
    c_p = _tile(s_len, GLA_CHUNK, SUBLANES)
    ob_p, st_p = _gla(z3, z3, z3, z3, g_all.reshape(1, n_all, h_b * dk_b)[:, :n_p], offs[3:7],
                      jnp.zeros((bp, h_b, dk_b, dv_b), F32), gla_norm_w[0], c_p)

    pos_s = jnp.tile(past + jnp.arange(l_new, dtype=jnp.int32), bs)
    q_s, kf_s, kb_s, vf_s, vb_s = _prep(z, n_p, n_s, pos_s, n_maps, dk_a, h_a * dv_a, offs[0:3])
    oa_s = _attn_sample(
        q_s.reshape(bs, l_new, -1), kb_s.reshape(bs, l_new, -1), vb_s.reshape(bs, l_new, -1),
        cache_k.reshape(n_pool, page * n_maps, dk_a), cache_v.reshape(n_pool, page * h_a, dv_a),
        page_table, lam_rows, sw, h_a, dk_a, dv_a)
    c_s = max(SUBLANES, 1 << (l_new - 1).bit_length())
    pad_s = lambda a: jnp.pad(a.reshape(bs, l_new, -1), ((0, 0), (0, c_s - l_new), (0, 0)))
    zs = pad_s(z[n_p:, offs[3]:offs[7]])
    o3 = offs[3]
    ob_s, st_s = _gla(zs, zs, zs, zs, pad_s(g_all[n_p:]), [o - o3 for o in offs[3:7]],
                      state_gla[0].astype(F32), gla_norm_w[0], c_s)

    oa = jnp.concatenate([oa_p, oa_s.reshape(n_s, -1).astype(BF16)], axis=0)
    ob = jnp.concatenate([ob_p.reshape(n_p, -1), ob_s[:, :l_new].reshape(n_s, -1)], axis=0)
    mix = _matmul([(oa, w_branch_a[0].astype(BF16), 0), (ob, w_branch_b[0].astype(BF16), 0)],
                  [(z, offs[7]), (z, offs[8])], d, _epi_gated_merge, BF16, "branch_merge")
    x1 = _matmul([(mix, w_out[0].astype(BF16), 0)], [(x_all, 0)], d, _epi_residual, F32, "out_proj")

    h, top_i, gates = _router(x1, norm_ffn_w[0], w_router[0], b_router[0])
    m = n_all * TOP_K
    g_rows = -(-int(math.ceil(MOE_GROUP_SLACK * m / n_experts)) // MOE_TM) * MOE_TM
    gather_rows = _tile(g_rows, GATHER_ROWS, BF16_SUBLANES)
    n_groups = -(-m // g_rows) + n_experts
    slot, slot_tok, tile_valid, g_exp, g_cnt, n_valid = _routing_tables(
        top_i[:, :TOP_K], n_experts, g_rows, n_groups, gather_rows)
    xs = _gather_rows(h, slot_tok, tile_valid, gather_rows)
    y_slots = _expert_ffn(xs.reshape(n_groups, g_rows, d), g_exp, g_cnt, n_valid,
                          w_gate_up[0], b_gate_up[0], w_down[0], b_down[0], MOE_TM)
    y_slots = y_slots.reshape(n_groups * g_rows, d)
    rows_c = _tile(math.gcd(n_p, n_s), COMBINE_ROWS, SUBLANES)
    pos_tiles = slot.reshape(n_all // rows_c, rows_c, TOP_K).transpose(0, 2, 1).reshape(n_all // rows_c, 1, TOP_K * rows_c)
    y_p = _combine(pos_tiles, y_slots, gates, x1, norm_final_w, 0, n_p, rows_c)
    y_s = _combine(pos_tiles, y_slots, gates, x1, norm_final_w, n_p, n_s, rows_c)

    return (
        y_p.reshape(bp, s_len, d),
        y_s.reshape(bs, l_new, d),
        kf_p.reshape(1, bp, s_len, n_maps, dk_a),
        vf_p.reshape(1, bp, s_len, h_a, dv_a),
        st_p.reshape(1, bp, h_b, dk_b, dv_b),
        kf_s.reshape(1, bs, l_new, n_maps, dk_a),
        vf_s.reshape(1, bs, l_new, h_a, dv_a),
        st_s.reshape(1, bs, h_b, dk_b, dv_b),
    )
```

```python
import functools
import math

import jax
import jax.numpy as jnp
import numpy as np
from jax import lax
from jax.experimental import pallas as pl
from jax.experimental.pallas import tpu as pltpu

F32 = jnp.float32
BF16 = jnp.bfloat16

ROPE_THETA = 500000.0
GLA_TAU = 16.0
TOP_K = 4
SWIGLU_LIMIT = 7.0
SWIGLU_ALPHA = 1.702
NORM_EPS = 1e-5
LAMBDA_INIT = 0.8 - 0.6 * math.exp(-0.3 * 0)

LANES = 128
SUBLANES = 8
BF16_SUBLANES = 16
VMEM_LIMIT_BYTES = 56 * 1024 * 1024

ROW_TILE = 256
MM_TM = 1024
MM_TN = 512
ATTN_TILE = 1024
ATTN_ROW_BLOCK = 128
PAGES_PER_STEP = 8
GLA_CHUNK = 128
MOE_TM = 128
MOE_TN = 256
MOE_TN_DOWN = 512
MOE_CAST_CHUNK = 1024
MOE_GROUP_SLACK = 1.2
GATHER_ROWS = 256
COMBINE_ROWS = 64


def _tile(n, pref, mult):
    best = None
    for d in range(mult, min(n, pref) + 1, mult):
        if n % d == 0:
            best = d
    return best if best is not None else n


def _params(sem, vmem=None):
    return pltpu.CompilerParams(dimension_semantics=sem, vmem_limit_bytes=vmem)


def _sigmoid(x):
    return 1.0 / (1.0 + jnp.exp(-x))


def _nt_dot(a, b):
    return lax.dot_general(a, b, (((1,), (1,)), ((), ())), preferred_element_type=F32)


def _tn_dot(a, b):
    return lax.dot_general(a, b, (((0,), (0,)), ((), ())), preferred_element_type=F32)


def _rmsnorm_body(x_ref, w_ref, o_ref):
    x = x_ref[...].astype(F32)
    y = x * lax.rsqrt(jnp.mean(x * x, axis=-1, keepdims=True) + NORM_EPS)
    o_ref[...] = (y * w_ref[...]).astype(o_ref.dtype)


def _rmsnorm(x, w, out_dtype):
    n, d = x.shape
    tr = _tile(n, ROW_TILE, BF16_SUBLANES)
    return pl.pallas_call(
        _rmsnorm_body,
        grid=(n // tr,),
        in_specs=[pl.BlockSpec((tr, d), lambda i: (i, 0)), pl.BlockSpec((1, d), lambda i: (0, 0))],
        out_specs=pl.BlockSpec((tr, d), lambda i: (i, 0)),
        out_shape=jax.ShapeDtypeStruct((n, d), out_dtype),
        compiler_params=_params(("parallel",)),
        name="rmsnorm",
    )(x, w.reshape(1, d).astype(F32))


def _mm_body(*refs, n_pairs, epilogue):
    o_ref = refs[-1]
    accs = [
        jnp.dot(refs[2 * p][...].astype(BF16), refs[2 * p + 1][...].astype(BF16), preferred_element_type=F32)
        for p in range(n_pairs)
    ]
    extras = [r[...] for r in refs[2 * n_pairs:-1]]
    o_ref[...] = epilogue(accs, extras).astype(o_ref.dtype)


def _matmul(pairs, extras, n_out, epilogue, out_dtype, name):
    m = pairs[0][0].shape[0]
    tm = _tile(m, MM_TM, BF16_SUBLANES)
    tn = _tile(n_out, MM_TN, LANES)
    in_specs, args = [], []
    for a, b, off in pairs:
        k = a.shape[1]
        assert off % tn == 0 and b.shape[0] == k
        in_specs.append(pl.BlockSpec((tm, k), lambda i, j: (i, 0)))
        in_specs.append(pl.BlockSpec((k, tn), functools.partial(lambda i, j, o: (0, j + o), o=off // tn)))
        args += [a, b]
    for e, off in extras:
        assert off % tn == 0
        in_specs.append(pl.BlockSpec((tm, tn), functools.partial(lambda i, j, o: (i, j + o), o=off // tn)))
        args.append(e)
    return pl.pallas_call(
        functools.partial(_mm_body, n_pairs=len(pairs), epilogue=epilogue),
        grid=(m // tm, n_out // tn),
        in_specs=in_specs,
        out_specs=pl.BlockSpec((tm, tn), lambda i, j: (i, j)),
        out_shape=jax.ShapeDtypeStruct((m, n_out), out_dtype),
        compiler_params=_params(("parallel", "arbitrary"), VMEM_LIMIT_BYTES),
        name=name,
    )(*args)


def _epi_plain(accs, extras):
    return accs[0]


def _epi_gated_merge(accs, extras):
    return _sigmoid(extras[0]) * accs[0] + _sigmoid(extras[1]) * accs[1]


def _epi_residual(accs, extras):
    return extras[0] + accs[0]


def _decay_body(u_ref, wlr_ref, wa2_ref, ba_ref, g_ref):
    a = jnp.dot(u_ref[...], wlr_ref[...], preferred_element_type=F32)
    x = jnp.dot(a.astype(BF16), wa2_ref[...], preferred_element_type=F32) + ba_ref[...]
    g_ref[...] = (jnp.minimum(x, 0.0) - jnp.log1p(jnp.exp(-jnp.abs(x)))) * (1.0 / GLA_TAU)


def _decay(u, w_lr, w_alpha2, b_alpha):
    n, d = u.shape
    rank, c = w_alpha2.shape
    assert rank <= LANES
    wlr = jnp.zeros((d, LANES), BF16).at[:, :rank].set(w_lr.astype(BF16))
    wa2 = jnp.zeros((LANES, c), BF16).at[:rank].set(w_alpha2.astype(BF16))
    tr = _tile(n, 2 * ROW_TILE, BF16_SUBLANES)
    return pl.pallas_call(
        _decay_body,
        grid=(n // tr,),
        in_specs=[
            pl.BlockSpec((tr, d), lambda i: (i, 0)),
            pl.BlockSpec((d, LANES), lambda i: (0, 0)),
            pl.BlockSpec((LANES, c), lambda i: (0, 0)),
            pl.BlockSpec((1, c), lambda i: (0, 0)),
        ],
        out_specs=pl.BlockSpec((tr, c), lambda i: (i, 0)),
        out_shape=jax.ShapeDtypeStruct((n, c), F32),
        compiler_params=_params(("parallel",)),
        name="gla_decay",
    )(u, wlr, wa2, b_alpha.reshape(1, c).astype(F32))


def _rope_tables(pos, dk):
    rot = dk // 4
    half = rot // 2
    inv_freq = jnp.power(jnp.float32(ROPE_THETA), -jnp.arange(half, dtype=F32) * (2.0 / rot))
    ang = pos.astype(F32)[:, None] * inv_freq[None, :]
    cos, sin = jnp.cos(ang), jnp.sin(ang)
    n = pos.shape[0]
    pad = jnp.zeros((n, dk - rot), F32)
    zero = jnp.zeros((n, half), F32)
    c = jnp.concatenate([cos, cos, pad + 1.0], axis=1)
    s1 = jnp.concatenate([-sin, zero, pad], axis=1)
    s2 = jnp.concatenate([zero, sin, pad], axis=1)
    return c, s1, s2


def _prep_body(zq_ref, zk_ref, zv_ref, c_ref, s1_ref, s2_ref, q_o, kf_o, kb_o, vf_o, vb_o, *, n_maps, dk):
    half = dk // 8
    c, s1, s2 = c_ref[...], s1_ref[...], s2_ref[...]
    q_scale = (dk ** -0.5) * math.log2(math.e)

    def rope(x):
        return x * c + pltpu.roll(x, dk - half, 1) * s1 + pltpu.roll(x, half, 1) * s2

    for m in range(n_maps):
        sl = slice(m * dk, (m + 1) * dk)
        q_o[:, sl] = (rope(zq_ref[:, sl]) * q_scale).astype(q_o.dtype)
        k = rope(zk_ref[:, sl])
        kf_o[:, sl] = k
        kb_o[:, sl] = k.astype(kb_o.dtype)
    v = zv_ref[...]
    vf_o[...] = v
    vb_o[...] = v.astype(vb_o.dtype)


def _prep(z, row0, n_rows, pos, n_maps, dk, w_v, offs):
    w_qk = n_maps * dk
    tr = _tile(n_rows, ROW_TILE, BF16_SUBLANES)
    assert row0 % tr == 0 and offs[0] % w_qk == 0 and offs[1] % w_qk == 0 and offs[2] % w_v == 0
    rb = row0 // tr
    c, s1, s2 = _rope_tables(pos, dk)
    zspec = lambda w, off: pl.BlockSpec((tr, w), functools.partial(lambda i, o: (i + rb, o), o=off // w))
    tspec = pl.BlockSpec((tr, dk), lambda i: (i, 0))
    ospec = lambda w: pl.BlockSpec((tr, w), lambda i: (i, 0))
    sds = lambda w, dt: jax.ShapeDtypeStruct((n_rows, w), dt)
    return pl.pallas_call(
        functools.partial(_prep_body, n_maps=n_maps, dk=dk),
        grid=(n_rows // tr,),
        in_specs=[zspec(w_qk, offs[0]), zspec(w_qk, offs[1]), zspec(w_v, offs[2]), tspec, tspec, tspec],
        out_specs=[ospec(w_qk), ospec(w_qk), ospec(w_qk), ospec(w_v), ospec(w_v)],
        out_shape=[sds(w_qk, BF16), sds(w_qk, F32), sds(w_qk, BF16), sds(w_v, F32), sds(w_v, BF16)],
        compiler_params=_params(("parallel",), VMEM_LIMIT_BYTES),
        name="qkv_rope",
    )(z, z, z, c, s1, s2)


def _lambda_value(lq1, lk1, lq2, lk2):
    a = jnp.exp(jnp.sum(lq1[...] * lk1[...], axis=-1, keepdims=True))
    b = jnp.exp(jnp.sum(lq2[...] * lk2[...], axis=-1, keepdims=True))
    return a - b + LAMBDA_INIT


def _diff_finish(a1, l1, a2, l2, lam, sw):
    o = a1 / l1 - lam * (a2 / l2)
    y = o * lax.rsqrt(jnp.mean(o * o, axis=-1, keepdims=True) + NORM_EPS)
    return (y * sw) * (1.0 - LAMBDA_INIT)


def _online_softmax_update(s, m_prev, l_prev):
    m_new = jnp.maximum(m_prev, jnp.max(s, axis=-1, keepdims=True))
    alpha = jnp.exp2(m_prev - m_new)
    p = jnp.exp2(s - m_new)
    return m_new, alpha, p, alpha * l_prev + jnp.sum(p, axis=-1, keepdims=True)


def _attn_prompt_body(qi_ref, kj_ref, q_ref, k_ref, v_ref, lq1, lk1, lq2, lk2, sw_ref, o_ref,
                      m_sc, l_sc, acc_sc, *, dk):
    t = pl.program_id(1)
    i = qi_ref[t]
    j = kj_ref[t]

    @pl.when(j == 0)
    def _():
        m_sc[...] = jnp.full(m_sc.shape, -jnp.inf, F32)
        l_sc[...] = jnp.zeros(l_sc.shape, F32)
        acc_sc[...] = jnp.zeros(acc_sc.shape, F32)

    t_blk, dv = acc_sc.shape[1], acc_sc.shape[2]
    rb = min(t_blk, ATTN_ROW_BLOCK)

    def lanes(x, width):
        return x if width == LANES else jnp.concatenate([x] * (width // LANES), axis=1)

    def step(masked):
        for c in range(2):
            for r in range(t_blk // rb):
                rows = slice(r * rb, (r + 1) * rb)
                n_k = (r + 1) * rb if masked else t_blk
                s = _nt_dot(q_ref[rows, c * dk:(c + 1) * dk], k_ref[0:n_k, c * dk:(c + 1) * dk])
                if masked:
                    row = lax.broadcasted_iota(jnp.int32, s.shape, 0) + r * rb
                    col = lax.broadcasted_iota(jnp.int32, s.shape, 1)
                    s = jnp.where(col <= row, s, -jnp.inf)
                m_prev = m_sc[c, rows, :]
                m_new = jnp.maximum(m_prev, jnp.max(s, axis=-1, keepdims=True))
                alpha = jnp.exp2(m_prev - m_new)
                p = jnp.exp2(s - lanes(m_new, n_k))
                l_sc[c, rows, :] = alpha * l_sc[c, rows, :] + jnp.sum(p, axis=-1, keepdims=True)
                acc_sc[c, rows, :] = lanes(alpha, dv) * acc_sc[c, rows, :] + jnp.dot(
                    p.astype(BF16), v_ref[0:n_k, :], preferred_element_type=F32)
                m_sc[c, rows, :] = m_new

    @pl.when(j < i)
    def _():
        step(False)

    @pl.when(j == i)
    def _():
        step(True)
        lam = _lambda_value(lq1, lk1, lq2, lk2)
        o_ref[...] = _diff_finish(acc_sc[0], l_sc[0][:, 0:1], acc_sc[1], l_sc[1][:, 0:1], lam,
                                  sw_ref[...]).astype(o_ref.dtype)


def _attn_prompt(q, k, v, lam_rows, subln_w, n_heads, dk, dv):
    s_len = q.shape[0]
    t_blk = _tile(s_len, ATTN_TILE, LANES)
    nb = s_len // t_blk
    qi = np.concatenate([np.full(i + 1, i, np.int32) for i in range(nb)])
    kj = np.concatenate([np.arange(i + 1, dtype=np.int32) for i in range(nb)])
    rowspec = pl.BlockSpec((1, dk), lambda h, t, qi, kj: (0, 0))
    grid_spec = pltpu.PrefetchScalarGridSpec(
        num_scalar_prefetch=2,
        grid=(n_heads, len(qi)),
        in_specs=[
            pl.BlockSpec((t_blk, 2 * dk), lambda h, t, qi, kj: (qi[t], h)),
            pl.BlockSpec((t_blk, 2 * dk), lambda h, t, qi, kj: (kj[t], h)),
            pl.BlockSpec((t_blk, dv), lambda h, t, qi, kj: (kj[t], h)),
            rowspec, rowspec, rowspec, rowspec,
            pl.BlockSpec((1, dv), lambda h, t, qi, kj: (0, 0)),
        ],
        out_specs=pl.BlockSpec((t_blk, dv), lambda h, t, qi, kj: (qi[t], h)),
        scratch_shapes=[
            pltpu.VMEM((2, t_blk, LANES), F32),
            pltpu.VMEM((2, t_blk, LANES), F32),
            pltpu.VMEM((2, t_blk, dv), F32),
        ],
    )
    return pl.pallas_call(
        functools.partial(_attn_prompt_body, dk=dk),
        grid_spec=grid_spec,
        out_shape=jax.ShapeDtypeStruct((s_len, n_heads * dv), BF16),
        compiler_params=_params(("parallel", "arbitrary"), VMEM_LIMIT_BYTES),
        name="diff_attn_prompt",
    )(jnp.asarray(qi), jnp.asarray(kj), q, k, v, *lam_rows, subln_w)


def _attn_sample_body(pt_ref, q_ref, *refs, n_pg, n_steps, n_heads, dv, lp, l_new):
    n_vc = dv // LANES
    k_refs, v_refs = refs[:n_pg], refs[n_pg:n_pg * (1 + n_vc)]
    kn_ref, vn_ref, lq1, lk1, lq2, lk2, sw_ref, o_ref, m_sc, l_sc, acc_sc = refs[n_pg * (1 + n_vc):]
    step = pl.program_id(1)

    @pl.when(step == 0)
    def _():
        m_sc[...] = jnp.full(m_sc.shape, -jnp.inf, F32)
        l_sc[...] = jnp.zeros(l_sc.shape, F32)
        acc_sc[...] = jnp.zeros(acc_sc.shape, F32)

    q = q_ref[...]

    def update(kb, vb, mask):
        s = _nt_dot(q, kb)
        if mask is not None:
            s = jnp.where(mask, s, -jnp.inf)
        m_new, alpha, p, l_next = _online_softmax_update(s, m_sc[...], l_sc[...])
        for h in range(n_heads):
            rows = slice(2 * h * lp, 2 * (h + 1) * lp)
            cols = slice(h * dv, (h + 1) * dv)
            acc_sc[rows, cols] = alpha[rows] * acc_sc[rows, cols] + jnp.dot(
                p[rows].astype(BF16), vb[:, cols], preferred_element_type=F32)
        m_sc[...] = m_new
        l_sc[...] = l_next

    @pl.when(step < n_steps)
    def _():
        def page_rows(refs_, groups):
            rows = refs_[0].shape[0] // groups
            return jnp.concatenate(
                [ref[pl.ds(gi, rows, stride=groups), :].astype(BF16) for gi in range(groups) for ref in refs_],
                axis=1)

        kb = jnp.concatenate([page_rows(k_refs[r:r + 1], 2 * n_heads) for r in range(n_pg)], axis=0)
        vb = jnp.concatenate([page_rows(v_refs[r * n_vc:(r + 1) * n_vc], n_heads) for r in range(n_pg)], axis=0)
        update(kb, vb, None)

    @pl.when(step == n_steps)
    def _():
        shape = (q.shape[0], kn_ref.shape[0])
        row_i = jnp.bitwise_and(lax.broadcasted_iota(jnp.int32, shape, 0), lp - 1)
        col = lax.broadcasted_iota(jnp.int32, shape, 1)
        update(kn_ref[...], vn_ref[...], jnp.logical_and(col < l_new, col <= row_i))
        lam = _lambda_value(lq1, lk1, lq2, lk2)
        for h in range(n_heads):
            r0 = 2 * h * lp
            cols = slice(h * dv, (h + 1) * dv)
            o_ref[:, cols] = _diff_finish(
                acc_sc[r0:r0 + lp, cols], l_sc[r0:r0 + lp], acc_sc[r0 + lp:r0 + 2 * lp, cols],
                l_sc[r0 + lp:r0 + 2 * lp], lam, sw_ref[...])


def _attn_sample(q_rot, k_new, v_new, cache_k, cache_v, page_table, lam_rows, subln_w, n_heads, dk, dv):
    b, l_new, w_qk = q_rot.shape
    n_maps = 2 * n_heads
    n_pages = page_table.shape[1]
    page = cache_v.shape[1] // n_heads
    w_v = n_heads * dv
    lp = max(SUBLANES, 1 << (l_new - 1).bit_length())
    assert l_new <= page
    qp = jnp.pad(q_rot.reshape(b, l_new, n_maps, dk), ((0, 0), (0, lp - l_new), (0, 0), (0, 0)))
    eye = jnp.eye(n_maps, dtype=BF16)
    q_bd = (qp.transpose(0, 2, 1, 3)[:, :, :, None, :] * eye[None, :, None, :, None]).reshape(b, n_maps * lp, w_qk)
    kn = jnp.pad(k_new, ((0, 0), (0, page - l_new), (0, 0)))
    vn = jnp.pad(v_new, ((0, 0), (0, page - l_new), (0, 0)))
    n_pg = _tile(n_pages, PAGES_PER_STEP, 1)
    n_steps = n_pages // n_pg

    assert dk == LANES and dv % LANES == 0
    n_vc = dv // LANES

    def page_map(r, cb=0):
        return lambda bi, s, pt: (pt[bi * n_pages + jnp.minimum(s, n_steps - 1) * n_pg + r], 0, cb)

    rowspec = pl.BlockSpec((1, dk), lambda bi, s, pt: (0, 0))
    seqspec = lambda rows, w: pl.BlockSpec((None, rows, w), lambda bi, s, pt: (bi, 0, 0))
    grid_spec = pltpu.PrefetchScalarGridSpec(
        num_scalar_prefetch=1,
        grid=(b, n_steps + 1),
        in_specs=(
            [seqspec(n_maps * lp, w_qk)]
            + [pl.BlockSpec((None, page * n_maps, dk), page_map(r)) for r in range(n_pg)]
            + [pl.BlockSpec((None, page * n_heads, LANES), page_map(r, cb))
               for r in range(n_pg) for cb in range(n_vc)]
            + [seqspec(page, w_qk), seqspec(page, w_v), rowspec, rowspec, rowspec, rowspec,
               pl.BlockSpec((1, dv), lambda bi, s, pt: (0, 0))]
        ),
        out_specs=seqspec(lp, w_v),
        scratch_shapes=[
            pltpu.VMEM((n_maps * lp, 1), F32),
            pltpu.VMEM((n_maps * lp, 1), F32),
            pltpu.VMEM((n_maps * lp, w_v), F32),
        ],
    )
    out = pl.pallas_call(
        functools.partial(_attn_sample_body, n_pg=n_pg, n_steps=n_steps, n_heads=n_heads, dv=dv, lp=lp,
                          l_new=l_new),
        grid_spec=grid_spec,
        out_shape=jax.ShapeDtypeStruct((b, lp, w_v), F32),
        compiler_params=_params(("parallel", "arbitrary"), VMEM_LIMIT_BYTES),
        name="diff_attn_sample",
    )(page_table.reshape(-1), q_bd, *([cache_k] * n_pg), *([cache_v] * (n_pg * n_vc)), kn, vn, *lam_rows, subln_w)
    return out[:, :l_new]


def _gla_constants(c):
    levels = c.bit_length() - 1
    assert 1 << levels == c
    idx = np.arange(c)
    i, t = idx[:, None], idx[None, :]
    sums = [t <= i, t > i]
    masks = [i == t]
    for lvl in range(1, levels + 1):
        blk, half = 1 << lvl, 1 << (lvl - 1)
        pos = idx % blk
        mid = (idx // blk) * blk + half - 1
        right, left = pos >= half, pos < half
        sums.append(right[:, None] & (t > mid[:, None]) & (t <= i))
        sums.append(left[:, None] & (t > i) & (t <= mid[:, None]))
        masks.append((idx[:, None] // blk == idx[None, :] // blk) & right[:, None] & left[None, :])
    return (jnp.asarray(np.concatenate(sums, 0).astype(np.float32), BF16),
            jnp.asarray(np.stack(masks, 0).astype(np.float32), F32))


def _gla_body(q_ref, k_ref, v_ref, r_ref, g_ref, s0_ref, sum_ref, mask_ref, gw_ref, o_ref, sout_ref, s_sc,
              *, c, levels, dk):
    ci = pl.program_id(2)

    @pl.when(ci == 0)
    def _():
        s_sc[...] = s0_ref[...].astype(F32)

    g = g_ref[...]
    g1 = g.astype(BF16)
    rem = g - g1.astype(F32)
    g2 = rem.astype(BF16)
    g3 = (rem - g2.astype(F32)).astype(BF16)
    sums = sum_ref[...]
    seg = (jnp.dot(sums, g1, preferred_element_type=F32) + jnp.dot(sums, g2, preferred_element_type=F32)
           + jnp.dot(sums, g3, preferred_element_type=F32))
    ones = jnp.ones((c, LANES), BF16)
    b_last = (_tn_dot(g1, ones) + _tn_dot(g2, ones) + _tn_dot(g3, ones))[:, 0:1]

    q = q_ref[...] * (dk ** -0.5)
    k = k_ref[...]
    v = v_ref[...].astype(BF16)
    state = s_sc[...]
    o = jnp.dot((q * jnp.exp(seg[0:c])).astype(BF16), state.astype(BF16), preferred_element_type=F32)
    a = mask_ref[0] * _nt_dot(q.astype(BF16), k.astype(BF16))
    for lvl in range(1, levels + 1):
        qd = (q * jnp.exp(seg[2 * lvl * c:(2 * lvl + 1) * c])).astype(BF16)
        kd = (k * jnp.exp(seg[(2 * lvl + 1) * c:(2 * lvl + 2) * c])).astype(BF16)
        a = a + mask_ref[lvl] * _nt_dot(qd, kd)
    o = o + jnp.dot(a.astype(BF16), v, preferred_element_type=F32)
    k_tail = (k * jnp.exp(seg[c:2 * c])).astype(BF16)
    new_state = state * jnp.exp(b_last) + _tn_dot(k_tail, v)
    s_sc[...] = new_state

    y = o * lax.rsqrt(jnp.mean(o * o, axis=-1, keepdims=True) + NORM_EPS) * gw_ref[...]
    r = r_ref[...]
    o_ref[...] = (y * (r * _sigmoid(r))).astype(o_ref.dtype)

    @pl.when(ci == pl.num_programs(2) - 1)
    def _():
        sout_ref[...] = new_state


def _gla(q, k, v, r, g, offs, s0, gla_norm_w, c):
    b, n_heads, dk, dv = s0.shape
    rows = g.shape[1]
    n_chunks = rows // c
    levels = c.bit_length() - 1
    sums, masks = _gla_constants(c)
    assert all(off % w == 0 for off, w in zip(offs, (dk, dk, dv, dv)))
    col = lambda w, off: pl.BlockSpec(
        (None, c, w), functools.partial(lambda bi, h, ci, o: (bi, ci, h + o), o=off // w))
    const3 = lambda shp: pl.BlockSpec(shp, lambda bi, h, ci: (0,) * len(shp))
    sspec = pl.BlockSpec((None, None, dk, dv), lambda bi, h, ci: (bi, h, 0, 0))
    return pl.pallas_call(
        functools.partial(_gla_body, c=c, levels=levels, dk=dk),
        grid=(b, n_heads, n_chunks),
        in_specs=[col(dk, offs[0]), col(dk, offs[1]), col(dv, offs[2]), col(dv, offs[3]), col(dk, 0), sspec,
                  const3(sums.shape), const3(masks.shape), const3((1, dv))],
        out_specs=[col(dv, 0), sspec],
        out_shape=[jax.ShapeDtypeStruct((b, rows, n_heads * dv), BF16),
                   jax.ShapeDtypeStruct((b, n_heads, dk, dv), F32)],
        scratch_shapes=[pltpu.VMEM((dk, dv), F32)],
        compiler_params=_params(("parallel", "parallel", "arbitrary"), VMEM_LIMIT_BYTES),
        name="gla",
    )(q, k, v, r, g, s0, sums, masks, gla_norm_w.reshape(1, dv).astype(F32))


def _router_body(x_ref, w_ref, wr_ref, br_ref, h_ref, idx_ref, gate_ref):
    x = x_ref[...]
    h = (x * lax.rsqrt(jnp.mean(x * x, axis=-1, keepdims=True) + NORM_EPS)) * w_ref[...]
    h_ref[...] = h
    logits = jnp.dot(h, wr_ref[...], precision=lax.Precision.HIGHEST, preferred_element_type=F32) + br_ref[...]
    lane = lax.broadcasted_iota(jnp.int32, logits.shape, 1)
    idx_out = jnp.zeros(logits.shape, jnp.int32)
    val_out = jnp.zeros(logits.shape, F32)
    top0 = None
    for kk in range(TOP_K):
        mx = jnp.max(logits, axis=-1, keepdims=True)
        idx = jnp.min(jnp.where(logits == mx, lane, LANES), axis=-1, keepdims=True)
        top0 = mx if top0 is None else top0
        idx_out = jnp.where(lane == kk, idx, idx_out)
        val_out = jnp.where(lane == kk, jnp.exp(mx - top0), val_out)
        logits = jnp.where(lane == idx, -jnp.inf, logits)
    idx_ref[...] = idx_out
    gate_ref[...] = val_out / jnp.sum(val_out, axis=-1, keepdims=True)


def _router(x, norm_w, w_router, b_router):
    n, d = x.shape
    e = w_router.shape[1]
    assert TOP_K <= e <= LANES
    wr = jnp.zeros((d, LANES), F32).at[:, :e].set(w_router.astype(F32))
    br = jnp.full((1, LANES), -jnp.inf, F32).at[0, :e].set(b_router.astype(F32))
    tr = _tile(n, ROW_TILE, SUBLANES)
    row = lambda w: pl.BlockSpec((tr, w), lambda i: (i, 0))
    fixed = lambda r, w: pl.BlockSpec((r, w), lambda i: (0, 0))
    return pl.pallas_call(
        _router_body,
        grid=(n // tr,),
        in_specs=[row(d), fixed(1, d), fixed(d, LANES), fixed(1, LANES)],
        out_specs=[row(d), row(LANES), row(LANES)],
        out_shape=[jax.ShapeDtypeStruct((n, d), F32), jax.ShapeDtypeStruct((n, LANES), jnp.int32),
                   jax.ShapeDtypeStruct((n, LANES), F32)],
        compiler_params=_params(("parallel",), VMEM_LIMIT_BYTES),
        name="ffn_norm_router",
    )(x, norm_w.reshape(1, d).astype(F32), wr, br)


def _row_gather(src_hbm, idx_ref, buf, sem, slot, n):
    def copy(src_row, r):
        return pltpu.make_async_copy(src_hbm.at[pl.ds(src_row, 1), :], buf.at[slot, pl.ds(r, 1), :], sem.at[slot])

    def issue_one(r, carry):
        copy(idx_ref[0, r], r).start()
        return carry

    def wait_one(r, carry):
        copy(0, r).wait()
        return carry

    def issue():
        lax.fori_loop(0, n, issue_one, 0, unroll=8)

    def wait():
        lax.fori_loop(0, n, wait_one, 0, unroll=8)

    return issue, wait


def _gather_body(valid_ref, tok_ref, tok_next_ref, h_hbm, o_ref, buf, sem, *, rows):
    t = pl.program_id(0)
    slot = t % 2
    issue_cur, wait_cur = _row_gather(h_hbm, tok_ref, buf, sem, slot, rows)
    issue_next, _ = _row_gather(h_hbm, tok_next_ref, buf, sem, 1 - slot, rows)

    @pl.when(jnp.logical_and(t == 0, valid_ref[0] > 0))
    def _():
        issue_cur()

    last = pl.num_programs(0) - 1

    @pl.when(jnp.logical_and(t < last, valid_ref[jnp.minimum(t + 1, last)] > 0))
    def _():
        issue_next()

    @pl.when(valid_ref[t] > 0)
    def _():
        wait_cur()
        o_ref[...] = buf[slot].astype(o_ref.dtype)

    @pl.when(valid_ref[t] == 0)
    def _():
        o_ref[...] = jnp.zeros(o_ref.shape, o_ref.dtype)


def _gather_rows(h, slot_tok, tile_valid, rows):
    n_slots = slot_tok.shape[0]
    d = h.shape[1]
    n_tiles = n_slots // rows
    grid_spec = pltpu.PrefetchScalarGridSpec(
        num_scalar_prefetch=1,
        grid=(n_tiles,),
        in_specs=[
            pl.BlockSpec((None, 1, rows), lambda t, valid: (t, 0, 0), memory_space=pltpu.SMEM),
            pl.BlockSpec((None, 1, rows), lambda t, valid: (jnp.minimum(t + 1, n_tiles - 1), 0, 0),
                         memory_space=pltpu.SMEM),
            pl.BlockSpec(memory_space=pl.ANY),
        ],
        out_specs=pl.BlockSpec((rows, d), lambda t, valid: (t, 0)),
        scratch_shapes=[pltpu.VMEM((2, rows, d), F32), pltpu.SemaphoreType.DMA((2,))],
    )
    return pl.pallas_call(
        functools.partial(_gather_body, rows=rows),
        grid_spec=grid_spec,
        out_shape=jax.ShapeDtypeStruct((n_slots, d), BF16),
        compiler_params=_params(("arbitrary",), VMEM_LIMIT_BYTES),
        name="moe_gather",
    )(tile_valid, slot_tok.reshape(n_tiles, 1, rows), slot_tok.reshape(n_tiles, 1, rows), h)


def _expert_rows(n_rows, x_ref, w_refs, w_sc, o_ref, tm, finish):
    k = w_sc.shape[0]
    kc = _tile(k, MOE_CAST_CHUNK, LANES)
    acc = None
    for c in range(k // kc):
        ks = slice(c * kc, (c + 1) * kc)
        w_c = jnp.concatenate([w[ks, :].astype(BF16) for w in w_refs], axis=1)
        w_sc[ks, :] = w_c
        part = jnp.dot(x_ref[0:tm, ks], w_c, preferred_element_type=F32)
        acc = part if acc is None else acc + part
    o_ref[0:tm, :] = finish(acc).astype(o_ref.dtype)

    def tile(start, size):
        rs = pl.ds(pl.multiple_of(start, tm), size)
        o_ref[rs, :] = finish(jnp.dot(x_ref[rs, :], w_sc[...], preferred_element_type=F32)).astype(o_ref.dtype)

    n_small = (n_rows + tm - 1) // tm
    rest = n_small - 1
    n_quad = rest // 4

    def quad_tile(r, carry):
        tile(tm + r * (4 * tm), 4 * tm)
        return carry

    def zero(r, carry):
        o_ref[pl.ds(pl.multiple_of(r * tm, tm), tm), :] = jnp.zeros((tm, o_ref.shape[1]), o_ref.dtype)
        return carry

    lax.fori_loop(0, n_quad, quad_tile, 0)
    base = tm + n_quad * (4 * tm)

    @pl.when(rest % 4 >= 2)
    def _():
        tile(base, 2 * tm)

    @pl.when(rest % 2 == 1)
    def _():
        tile(base + (rest % 4 // 2) * (2 * tm), tm)

    lax.fori_loop(n_small, o_ref.shape[0] // tm, zero, 0)


def _expert_up_body(ge_ref, gr_ref, nv_ref, x_ref, wg_ref, wu_ref, bg_ref, bu_ref, o_ref, w_sc, *, tm):
    g = pl.program_id(0)
    tn = o_ref.shape[1]

    @pl.when(g < nv_ref[0])
    def _():
        bias = jnp.concatenate([bg_ref[...], bu_ref[...]], axis=1)

        def swiglu(gu):
            gu = gu + bias
            gt = jnp.minimum(gu[:, :tn], SWIGLU_LIMIT)
            up = jnp.clip(gu[:, tn:], -SWIGLU_LIMIT, SWIGLU_LIMIT)
            return (up + 1.0) * (gt * _sigmoid(SWIGLU_ALPHA * gt))

        _expert_rows(gr_ref[g], x_ref, (wg_ref, wu_ref), w_sc, o_ref, tm, swiglu)


def _expert_down_body(ge_ref, gr_ref, nv_ref, x_ref, w_ref, b_ref, o_ref, w_sc, *, tm):
    g = pl.program_id(0)

    @pl.when(g < nv_ref[0])
    def _():
        _expert_rows(gr_ref[g], x_ref, (w_ref,), w_sc, o_ref, tm, lambda y: y + b_ref[...])


def _expert_ffn(xs, group_expert, group_rows, n_valid, w_gate_up, b_gate_up, w_down, b_down, tm):
    n_groups, g_rows, d = xs.shape
    e, _, two_de = w_gate_up.shape
    de = two_de // 2
    tn_u = _tile(de, MOE_TN, LANES)
    tn_d = _tile(d, MOE_TN_DOWN, LANES)
    nj_u, nj_d = de // tn_u, d // tn_d

    def maps(nj):
        gv = lambda g, nv: jnp.minimum(g, nv[0] - 1)
        jv = lambda g, j, nv: jnp.where(g < nv[0], j, nj - 1)
        return gv, jv

    gv, jv = maps(nj_u)
    act = pl.pallas_call(
        functools.partial(_expert_up_body, tm=tm),
        grid_spec=pltpu.PrefetchScalarGridSpec(
            num_scalar_prefetch=3,
            grid=(n_groups, nj_u),
            in_specs=[
                pl.BlockSpec((None, g_rows, d), lambda g, j, ge, gr, nv: (gv(g, nv), 0, 0)),
                pl.BlockSpec((None, d, tn_u), lambda g, j, ge, gr, nv: (ge[gv(g, nv)], 0, jv(g, j, nv))),
                pl.BlockSpec((None, d, tn_u), lambda g, j, ge, gr, nv: (ge[gv(g, nv)], 0, nj_u + jv(g, j, nv))),
                pl.BlockSpec((None, 1, tn_u), lambda g, j, ge, gr, nv: (ge[gv(g, nv)], 0, jv(g, j, nv))),
                pl.BlockSpec((None, 1, tn_u), lambda g, j, ge, gr, nv: (ge[gv(g, nv)], 0, nj_u + jv(g, j, nv))),
            ],
            out_specs=pl.BlockSpec((None, g_rows, tn_u), lambda g, j, ge, gr, nv: (gv(g, nv), 0, jv(g, j, nv))),
            scratch_shapes=[pltpu.VMEM((d, 2 * tn_u), BF16)],
        ),
        out_shape=jax.ShapeDtypeStruct((n_groups, g_rows, de), BF16),
        compiler_params=_params(("arbitrary", "arbitrary"), VMEM_LIMIT_BYTES),
        name="moe_gate_up",
    )(group_expert, group_rows, n_valid, xs, w_gate_up, w_gate_up, b_gate_up.reshape(e, 1, two_de),
      b_gate_up.reshape(e, 1, two_de))

    gv, jv = maps(nj_d)
    return pl.pallas_call(
        functools.partial(_expert_down_body, tm=tm),
        grid_spec=pltpu.PrefetchScalarGridSpec(
            num_scalar_prefetch=3,
            grid=(n_groups, nj_d),
            in_specs=[
                pl.BlockSpec((None, g_rows, de), lambda g, j, ge, gr, nv: (gv(g, nv), 0, 0)),
                pl.BlockSpec((None, de, tn_d), lambda g, j, ge, gr, nv: (ge[gv(g, nv)], 0, jv(g, j, nv))),
                pl.BlockSpec((None, 1, tn_d), lambda g, j, ge, gr, nv: (ge[gv(g, nv)], 0, jv(g, j, nv))),
            ],
            out_specs=pl.BlockSpec((None, g_rows, tn_d), lambda g, j, ge, gr, nv: (gv(g, nv), 0, jv(g, j, nv))),
            scratch_shapes=[pltpu.VMEM((de, tn_d), BF16)],
        ),
        out_shape=jax.ShapeDtypeStruct((n_groups, g_rows, d), F32),
        compiler_params=_params(("arbitrary", "arbitrary"), VMEM_LIMIT_BYTES),
        name="moe_down",
    )(group_expert, group_rows, n_valid, act, w_down, b_down.reshape(e, 1, d))


def _combine_body(pos_ref, pos_next_ref, y_hbm, gate_ref, x_ref, w_ref, o_ref, buf, sem, *, rows):
    t = pl.program_id(0)
    slot = t % 2
    issue_cur, wait_cur = _row_gather(y_hbm, pos_ref, buf, sem, slot, TOP_K * rows)
    issue_next, _ = _row_gather(y_hbm, pos_next_ref, buf, sem, 1 - slot, TOP_K * rows)

    @pl.when(t == 0)
    def _():
        issue_cur()

    @pl.when(t < pl.num_programs(0) - 1)
    def _():
        issue_next()

    wait_cur()
    gates = gate_ref[...]
    x = x_ref[...]
    for kk in range(TOP_K):
        x = x + gates[:, kk:kk + 1] * buf[slot, kk * rows:(kk + 1) * rows, :]
    y = x * lax.rsqrt(jnp.mean(x * x, axis=-1, keepdims=True) + NORM_EPS)
    o_ref[...] = y * w_ref[...]


def _combine(pos_tiles, y_slots, gates, x1, norm_w, row0, n_rows, rows):
    d = x1.shape[1]
    assert row0 % rows == 0 and n_rows % rows == 0
    rb = row0 // rows
    n_tiles = n_rows // rows
    grid_spec = pltpu.PrefetchScalarGridSpec(
        num_scalar_prefetch=0,
        grid=(n_tiles,),
        in_specs=[
            pl.BlockSpec((None, 1, TOP_K * rows), lambda t: (t + rb, 0, 0), memory_space=pltpu.SMEM),
            pl.BlockSpec((None, 1, TOP_K * rows), lambda t: (jnp.minimum(t + 1, n_tiles - 1) + rb, 0, 0),
                         memory_space=pltpu.SMEM),
            pl.BlockSpec(memory_space=pl.ANY),
            pl.BlockSpec((rows, LANES), lambda t: (t + rb, 0)),
            pl.BlockSpec((rows, d), lambda t: (t + rb, 0)),
            pl.BlockSpec((1, d), lambda t: (0, 0)),
        ],
        out_specs=pl.BlockSpec((rows, d), lambda t: (t, 0)),
        scratch_shapes=[pltpu.VMEM((2, TOP_K * rows, d), F32), pltpu.SemaphoreType.DMA((2,))],
    )
    return pl.pallas_call(
        functools.partial(_combine_body, rows=rows),
        grid_spec=grid_spec,
        out_shape=jax.ShapeDtypeStruct((n_rows, d), F32),
        compiler_params=_params(("arbitrary",), VMEM_LIMIT_BYTES),
        name="moe_combine_norm",
    )(pos_tiles, pos_tiles, y_slots, gates, x1, norm_w.reshape(1, d).astype(F32))


def _routing_tables(expert_ids, n_experts, g_rows, n_groups, gather_rows):
    n = expert_ids.shape[0]
    m = n * TOP_K
    e_flat = expert_ids.reshape(-1)
    onehot = (e_flat[:, None] == jnp.arange(n_experts, dtype=jnp.int32)[None, :]).astype(jnp.int32)
    before = jnp.cumsum(onehot, axis=0) - onehot
    rank = jnp.take_along_axis(before, e_flat[:, None], axis=1)[:, 0]
    counts = jnp.sum(onehot, axis=0)
    groups_per_e = (counts + g_rows - 1) // g_rows
    g_end = jnp.cumsum(groups_per_e)
    g_base = g_end - groups_per_e
    slot = (g_base[e_flat] + rank // g_rows) * g_rows + rank % g_rows
    n_valid = g_end[-1]
    gid = jnp.arange(n_groups, dtype=jnp.int32)
    g_exp = jnp.minimum(jnp.searchsorted(g_end, gid, side="right"), n_experts - 1).astype(jnp.int32)
    g_cnt = jnp.clip(counts[g_exp] - (gid - g_base[g_exp]) * g_rows, 0, g_rows)
    g_cnt = jnp.where(gid < n_valid, g_cnt, 0).astype(jnp.int32)
    slot_tok = jnp.zeros((n_groups * g_rows,), jnp.int32).at[slot].set(jnp.arange(m, dtype=jnp.int32) // TOP_K)
    tiles_per_g = g_rows // gather_rows
    tile_start = (jnp.arange(n_groups * tiles_per_g, dtype=jnp.int32) % tiles_per_g) * gather_rows
    tile_valid = (tile_start < jnp.repeat(g_cnt, tiles_per_g)).astype(jnp.int32)
    return slot.astype(jnp.int32), slot_tok, tile_valid, g_exp, g_cnt, n_valid.astype(jnp.int32).reshape(1)


def kernel(x_prompt, x_sample, cache_k, cache_v, state_gla, page_table, norm_mix_w, w_in, lambda_q1, lambda_k1, lambda_q2, lambda_k2, subln_w, w_alpha2, b_alpha, gla_norm_w, w_branch_a, w_branch_b, w_out, norm_ffn_w, w_router, b_router, w_gate_up, b_gate_up, w_down, b_down, norm_final_w):
    bp, s_len, d = x_prompt.shape
    bs, l_new, _ = x_sample.shape
    depth, n_pool, page, n_maps, dk_a = cache_k.shape
    _, _, _, h_a, dv_a = cache_v.shape
    _, _, h_b, dk_b, dv_b = state_gla.shape
    rank = w_alpha2.shape[1]
    n_experts = w_router.shape[2]
    assert depth == 1 and bp == 1 and n_maps == 2 * h_a
    n_p, n_s = bp * s_len, bs * l_new
    n_all = n_p + n_s
    sizes = (n_maps * dk_a, n_maps * dk_a, h_a * dv_a, h_b * dk_b, h_b * dk_b, h_b * dv_b, h_b * dv_b, d, d, rank)
    offs = [int(o) for o in np.cumsum((0,) + sizes)]
    n_main = offs[9]
    assert w_in.shape[2] == offs[10]
    past = page_table.shape[1] * page

    x_all = jnp.concatenate([x_prompt.reshape(n_p, d), x_sample.reshape(n_s, d)], axis=0)
    u = _rmsnorm(x_all, norm_mix_w[0], BF16)
    w_in_b = w_in[0].astype(BF16)
    z = _matmul([(u, w_in_b, 0)], [], n_main, _epi_plain, F32, "in_proj")
    g_all = _decay(u, w_in_b[:, n_main:], w_alpha2[0], b_alpha[0])

    lam_rows = [p[0].reshape(1, dk_a).astype(F32) for p in (lambda_q1, lambda_k1, lambda_q2, lambda_k2)]
    sw = subln_w[0].reshape(1, dv_a).astype(F32)

    pos_p = jnp.arange(s_len, dtype=jnp.int32)
    q_p, kf_p, kb_p, vf_p, vb_p = _prep(z, 0, n_p, pos_p, n_maps, dk_a, h_a * dv_a, offs[0:3])
    oa_p = _attn_prompt(q_p, kb_p, vb_p, lam_rows, sw, h_a, dk_a, dv_a)
    z3 = z.reshape(1, n_all, n_main)
    c_p = _tile(s_len, GLA_CHUNK, SUBLANES)
    ob_p, st_p = _gla(z3, z3, z3, z3, g_all.reshape(1, n_all, h_b * dk_b)[:, :n_p], offs[3:7],
                      jnp.zeros((bp, h_b, dk_b, dv_b), F32), gla_norm_w[0], c_p)

    pos_s = jnp.tile(past + jnp.arange(l_new, dtype=jnp.int32), bs)
    q_s, kf_s, kb_s, vf_s, vb_s = _prep(z, n_p, n_s, pos_s, n_maps, dk_a, h_a * dv_a, offs[0:3])
    oa_s = _attn_sample(
        q_s.reshape(bs, l_new, -1), kb_s.reshape(bs, l_new, -1), vb_s.reshape(bs, l_new, -1),
        cache_k.reshape(n_pool, page * n_maps, dk_a), cache_v.reshape(n_pool, page * h_a, dv_a),
        page_table, lam_rows, sw, h_a, dk_a, dv_a)
    c_s = max(SUBLANES, 1 << (l_new - 1).bit_length())
    pad_s = lambda a: jnp.pad(a.reshape(bs, l_new, -1), ((0, 0), (0, c_s - l_new), (0, 0)))
    zs = pad_s(z[n_p:, offs[3]:offs[7]])
    o3 = offs[3]
    ob_s, st_s = _gla(zs, zs, zs, zs, pad_s(g_all[n_p:]), [o - o3 for o in offs[3:7]],
                      state_gla[0].astype(F32), gla_norm_w[0], c_s)

    oa = jnp.concatenate([oa_p, oa_s.reshape(n_s, -1).astype(BF16)], axis=0)
    ob = jnp.concatenate([ob_p.reshape(n_p, -1), ob_s[:, :l_new].reshape(n_s, -1)], axis=0)
    mix = _matmul([(oa, w_branch_a[0].astype(BF16), 0), (ob, w_branch_b[0].astype(BF16), 0)],
                  [(z, offs[7]), (z, offs[8])], d, _epi_gated_merge, BF16, "branch_merge")
    x1 = _matmul([(mix, w_out[0].astype(BF16), 0)], [(x_all, 0)], d, _epi_residual, F32, "out_proj")

    h, top_i, gates = _router(x1, norm_ffn_w[0], w_router[0], b_router[0])
    m = n_all * TOP_K
    g_rows = -(-int(math.ceil(MOE_GROUP_SLACK * m / n_experts)) // MOE_TM) * MOE_TM
    gather_rows = _tile(g_rows, GATHER_ROWS, BF16_SUBLANES)
    n_groups = -(-m // g_rows) + n_experts
    slot, slot_tok, tile_valid, g_exp, g_cnt, n_valid = _routing_tables(
        top_i[:, :TOP_K], n_experts, g_rows, n_groups, gather_rows)
    xs = _gather_rows(h, slot_tok, tile_valid, gather_rows)
    y_slots = _expert_ffn(xs.reshape(n_groups, g_rows, d), g_exp, g_cnt, n_valid,
                          w_gate_up[0], b_gate_up[0], w_down[0], b_down[0], MOE_TM)
    y_slots = y_slots.reshape(n_groups * g_rows, d)
    rows_c = _tile(math.gcd(n_p, n_s), COMBINE_ROWS, SUBLANES)
    pos_tiles = slot.reshape(n_all // rows_c, rows_c, TOP_K).transpose(0, 2, 1).reshape(n_all // rows_c, 1, TOP_K * rows_c)
    y_p = _combine(pos_tiles, y_slots, gates, x1, norm_final_w, 0, n_p, rows_c)
    y_s = _combine(pos_tiles, y_slots, gates, x1, norm_final_w, n_p, n_s, rows_c)

    return (
        y_p.reshape(bp, s_len, d),
        y_s.reshape(bs, l_new, d),
        kf_p.reshape(1, bp, s_len, n_maps, dk_a),
        vf_p.reshape(1, bp, s_len, h_a, dv_a),
        st_p.reshape(1, bp, h_b, dk_b, dv_b),
        kf_s.reshape(1, bs, l_new, n_maps, dk_a),
        vf_s.reshape(1, bs, l_new, h_a, dv_a),
        st_s.reshape(1, bs, h_b, dk_b, dv_b),
    )
```

```python
import functools
import math

import jax
import jax.numpy as jnp
import numpy as np
from jax import lax
from jax.experimental import pallas as pl
from jax.experimental.pallas import tpu as pltpu

F32 = jnp.float32
BF16 = jnp.bfloat16

ROPE_THETA = 500000.0
GLA_TAU = 16.0
TOP_K = 4
SWIGLU_LIMIT = 7.0
SWIGLU_ALPHA = 1.702
NORM_EPS = 1e-5
LAMBDA_INIT = 0.8 - 0.6 * math.exp(-0.3 * 0)

LANES = 128
SUBLANES = 8
BF16_SUBLANES = 16
VMEM_LIMIT_BYTES = 56 * 1024 * 1024

ROW_TILE = 256
MM_TM = 1024
MM_TM_WIDE = 2048
MM_TN = 512
ATTN_TILE = 1024
ATTN_ROW_BLOCK = 128
PAGES_PER_STEP = 8
GLA_CHUNK = 128
MOE_TM = 128
MOE_TN = 256
MOE_TN_DOWN = 512
MOE_CAST_CHUNK = 1024
MOE_GROUP_SLACK = 1.45
GATHER_ROWS = 128
COMBINE_ROWS = 64


def _tile(n, pref, mult):
    best = None
    for d in range(mult, min(n, pref) + 1, mult):
        if n % d == 0:
            best = d
    return best if best is not None else n


def _params(sem, vmem=None):
    return pltpu.CompilerParams(dimension_semantics=sem, vmem_limit_bytes=vmem)


def _sigmoid(x):
    return 1.0 / (1.0 + jnp.exp(-x))


def _nt_dot(a, b):
    return lax.dot_general(a, b, (((1,), (1,)), ((), ())), preferred_element_type=F32)


def _tn_dot(a, b):
    return lax.dot_general(a, b, (((0,), (0,)), ((), ())), preferred_element_type=F32)


def _rmsnorm_body(x_ref, w_ref, o_ref):
    x = x_ref[...].astype(F32)
    y = x * lax.rsqrt(jnp.mean(x * x, axis=-1, keepdims=True) + NORM_EPS)
    o_ref[...] = (y * w_ref[...]).astype(o_ref.dtype)


def _rmsnorm(x, w, out_dtype):
    n, d = x.shape
    tr = _tile(n, ROW_TILE, BF16_SUBLANES)
    return pl.pallas_call(
        _rmsnorm_body,
        grid=(n // tr,),
        in_specs=[pl.BlockSpec((tr, d), lambda i: (i, 0)), pl.BlockSpec((1, d), lambda i: (0, 0))],
        out_specs=pl.BlockSpec((tr, d), lambda i: (i, 0)),
        out_shape=jax.ShapeDtypeStruct((n, d), out_dtype),
        compiler_params=_params(("parallel",)),
        name="rmsnorm",
    )(x, w.reshape(1, d).astype(F32))


def _mm_body(*refs, n_pairs, epilogue):
    o_ref = refs[-1]
    accs = [
        jnp.dot(refs[2 * p][...].astype(BF16), refs[2 * p + 1][...].astype(BF16), preferred_element_type=F32)
        for p in range(n_pairs)
    ]
    extras = [r[...] for r in refs[2 * n_pairs:-1]]
    o_ref[...] = epilogue(accs, extras).astype(o_ref.dtype)


def _matmul(pairs, extras, n_out, epilogue, out_dtype, name, tm_pref=MM_TM):
    m = pairs[0][0].shape[0]
    tm = _tile(m, tm_pref, BF16_SUBLANES)
    tn = _tile(n_out, MM_TN, LANES)
    in_specs, args = [], []
    for a, b, off in pairs:
        k = a.shape[1]
        assert off % tn == 0 and b.shape[0] == k
        in_specs.append(pl.BlockSpec((tm, k), lambda i, j: (i, 0)))
        in_specs.append(pl.BlockSpec((k, tn), functools.partial(lambda i, j, o: (0, j + o), o=off // tn)))
        args += [a, b]
    for e, off in extras:
        assert off % tn == 0
        in_specs.append(pl.BlockSpec((tm, tn), functools.partial(lambda i, j, o: (i, j + o), o=off // tn)))
        args.append(e)
    return pl.pallas_call(
        functools.partial(_mm_body, n_pairs=len(pairs), epilogue=epilogue),
        grid=(m // tm, n_out // tn),
        in_specs=in_specs,
        out_specs=pl.BlockSpec((tm, tn), lambda i, j: (i, j)),
        out_shape=jax.ShapeDtypeStruct((m, n_out), out_dtype),
        compiler_params=_params(("parallel", "arbitrary"), VMEM_LIMIT_BYTES),
        name=name,
    )(*args)


def _epi_plain(accs, extras):
    return accs[0]


def _epi_gated_merge(accs, extras):
    return _sigmoid(extras[0]) * accs[0] + _sigmoid(extras[1]) * accs[1]


def _epi_residual(accs, extras):
    return extras[0] + accs[0]


def _decay_body(u_ref, wlr_ref, wa2_ref, ba_ref, g_ref):
    a = jnp.dot(u_ref[...], wlr_ref[...], preferred_element_type=F32)
    x = jnp.dot(a.astype(BF16), wa2_ref[...], preferred_element_type=F32) + ba_ref[...]
    g_ref[...] = (jnp.minimum(x, 0.0) - jnp.log1p(jnp.exp(-jnp.abs(x)))) * (1.0 / GLA_TAU)


def _decay(u, w_lr, w_alpha2, b_alpha):
    n, d = u.shape
    rank, c = w_alpha2.shape
    assert rank <= LANES
    wlr = jnp.zeros((d, LANES), BF16).at[:, :rank].set(w_lr.astype(BF16))
    wa2 = jnp.zeros((LANES, c), BF16).at[:rank].set(w_alpha2.astype(BF16))
    tr = _tile(n, 2 * ROW_TILE, BF16_SUBLANES)
    return pl.pallas_call(
        _decay_body,
        grid=(n // tr,),
        in_specs=[
            pl.BlockSpec((tr, d), lambda i: (i, 0)),
            pl.BlockSpec((d, LANES), lambda i: (0, 0)),
            pl.BlockSpec((LANES, c), lambda i: (0, 0)),
            pl.BlockSpec((1, c), lambda i: (0, 0)),
        ],
        out_specs=pl.BlockSpec((tr, c), lambda i: (i, 0)),
        out_shape=jax.ShapeDtypeStruct((n, c), F32),
        compiler_params=_params(("parallel",)),
        name="gla_decay",
    )(u, wlr, wa2, b_alpha.reshape(1, c).astype(F32))


def _rope_tables(pos, dk):
    rot = dk // 4
    half = rot // 2
    inv_freq = jnp.power(jnp.float32(ROPE_THETA), -jnp.arange(half, dtype=F32) * (2.0 / rot))
    ang = pos.astype(F32)[:, None] * inv_freq[None, :]
    cos, sin = jnp.cos(ang), jnp.sin(ang)
    n = pos.shape[0]
    pad = jnp.zeros((n, dk - rot), F32)
    zero = jnp.zeros((n, half), F32)
    c = jnp.concatenate([cos, cos, pad + 1.0], axis=1)
    s1 = jnp.concatenate([-sin, zero, pad], axis=1)
    s2 = jnp.concatenate([zero, sin, pad], axis=1)
    return c, s1, s2


def _prep_body(zq_ref, zk_ref, zv_ref, c_ref, s1_ref, s2_ref, q_o, kf_o, kb_o, vf_o, vb_o, *, n_maps, dk):
    half = dk // 8
    c, s1, s2 = c_ref[...], s1_ref[...], s2_ref[...]
    q_scale = (dk ** -0.5) * math.log2(math.e)

    def rope(x):
        return x * c + pltpu.roll(x, dk - half, 1) * s1 + pltpu.roll(x, half, 1) * s2

    for m in range(n_maps):
        sl = slice(m * dk, (m + 1) * dk)
        q_o[:, sl] = (rope(zq_ref[:, sl]) * q_scale).astype(q_o.dtype)
        k = rope(zk_ref[:, sl])
        kf_o[:, sl] = k
        kb_o[:, sl] = k.astype(kb_o.dtype)
    v = zv_ref[...]
    vf_o[...] = v
    vb_o[...] = v.astype(vb_o.dtype)


def _prep(z, row0, n_rows, pos, n_maps, dk, w_v, offs):
    w_qk = n_maps * dk
    tr = _tile(n_rows, ROW_TILE, BF16_SUBLANES)
    assert row0 % tr == 0 and offs[0] % w_qk == 0 and offs[1] % w_qk == 0 and offs[2] % w_v == 0
    rb = row0 // tr
    c, s1, s2 = _rope_tables(pos, dk)
    zspec = lambda w, off: pl.BlockSpec((tr, w), functools.partial(lambda i, o: (i + rb, o), o=off // w))
    tspec = pl.BlockSpec((tr, dk), lambda i: (i, 0))
    ospec = lambda w: pl.BlockSpec((tr, w), lambda i: (i, 0))
    sds = lambda w, dt: jax.ShapeDtypeStruct((n_rows, w), dt)
    return pl.pallas_call(
        functools.partial(_prep_body, n_maps=n_maps, dk=dk),
        grid=(n_rows // tr,),
        in_specs=[zspec(w_qk, offs[0]), zspec(w_qk, offs[1]), zspec(w_v, offs[2]), tspec, tspec, tspec],
        out_specs=[ospec(w_qk), ospec(w_qk), ospec(w_qk), ospec(w_v), ospec(w_v)],
        out_shape=[sds(w_qk, BF16), sds(w_qk, F32), sds(w_qk, BF16), sds(w_v, F32), sds(w_v, BF16)],
        compiler_params=_params(("parallel",), VMEM_LIMIT_BYTES),
        name="qkv_rope",
    )(z, z, z, c, s1, s2)


def _lambda_value(lq1, lk1, lq2, lk2):
    a = jnp.exp(jnp.sum(lq1[...] * lk1[...], axis=-1, keepdims=True))
    b = jnp.exp(jnp.sum(lq2[...] * lk2[...], axis=-1, keepdims=True))
    return a - b + LAMBDA_INIT


def _diff_finish(a1, l1, a2, l2, lam, sw):
    o = a1 / l1 - lam * (a2 / l2)
    y = o * lax.rsqrt(jnp.mean(o * o, axis=-1, keepdims=True) + NORM_EPS)
    return (y * sw) * (1.0 - LAMBDA_INIT)


def _online_softmax_update(s, m_prev, l_prev):
    m_new = jnp.maximum(m_prev, jnp.max(s, axis=-1, keepdims=True))
    alpha = jnp.exp2(m_prev - m_new)
    p = jnp.exp2(s - m_new)
    return m_new, alpha, p, alpha * l_prev + jnp.sum(p, axis=-1, keepdims=True)


def _attn_prompt_body(qi_ref, kj_ref, q_ref, k_ref, v_ref, lq1, lk1, lq2, lk2, sw_ref, o_ref,
                      m_sc, l_sc, acc_sc, *, dk):
    t = pl.program_id(1)
    i = qi_ref[t]
    j = kj_ref[t]

    @pl.when(j == 0)
    def _():
        m_sc[...] = jnp.full(m_sc.shape, -jnp.inf, F32)
        l_sc[...] = jnp.zeros(l_sc.shape, F32)
        acc_sc[...] = jnp.zeros(acc_sc.shape, F32)

    t_blk, dv = acc_sc.shape[1], acc_sc.shape[2]
    rb = min(t_blk, ATTN_ROW_BLOCK)

    def lanes(x, width):
        return x if width == LANES else jnp.concatenate([x] * (width // LANES), axis=1)

    def step(masked):
        for c in range(2):
            for r in range(t_blk // rb):
                rows = slice(r * rb, (r + 1) * rb)
                n_k = min(t_blk, -(-(r + 1) * rb // LANES) * LANES) if masked else t_blk
                s = _nt_dot(q_ref[rows, c * dk:(c + 1) * dk], k_ref[0:n_k, c * dk:(c + 1) * dk])
                if masked:
                    row = lax.broadcasted_iota(jnp.int32, s.shape, 0) + r * rb
                    col = lax.broadcasted_iota(jnp.int32, s.shape, 1)
                    s = jnp.where(col <= row, s, -jnp.inf)
                m_prev = m_sc[c, rows, :]
                m_new = jnp.maximum(m_prev, jnp.max(s, axis=-1, keepdims=True))
                alpha = jnp.exp2(m_prev - m_new)
                p = jnp.exp2(s - lanes(m_new, n_k))
                l_sc[c, rows, :] = alpha * l_sc[c, rows, :] + jnp.sum(p, axis=-1, keepdims=True)
                acc_sc[c, rows, :] = lanes(alpha, dv) * acc_sc[c, rows, :] + jnp.dot(
                    p.astype(BF16), v_ref[0:n_k, :], preferred_element_type=F32)
                m_sc[c, rows, :] = m_new

    @pl.when(j < i)
    def _():
        step(False)

    @pl.when(j == i)
    def _():
        step(True)
        lam = _lambda_value(lq1, lk1, lq2, lk2)
        o_ref[...] = _diff_finish(acc_sc[0], l_sc[0][:, 0:1], acc_sc[1], l_sc[1][:, 0:1], lam,
                                  sw_ref[...]).astype(o_ref.dtype)


def _attn_prompt(q, k, v, lam_rows, subln_w, n_heads, dk, dv):
    s_len = q.shape[0]
    t_blk = _tile(s_len, ATTN_TILE, LANES)
    nb = s_len // t_blk
    qi = np.concatenate([np.full(i + 1, i, np.int32) for i in range(nb)])
    kj = np.concatenate([np.arange(i + 1, dtype=np.int32) for i in range(nb)])
    rowspec = pl.BlockSpec((1, dk), lambda h, t, qi, kj: (0, 0))
    grid_spec = pltpu.PrefetchScalarGridSpec(
        num_scalar_prefetch=2,
        grid=(n_heads, len(qi)),
        in_specs=[
            pl.BlockSpec((t_blk, 2 * dk), lambda h, t, qi, kj: (qi[t], h)),
            pl.BlockSpec((t_blk, 2 * dk), lambda h, t, qi, kj: (kj[t], h)),
            pl.BlockSpec((t_blk, dv), lambda h, t, qi, kj: (kj[t], h)),
            rowspec, rowspec, rowspec, rowspec,
            pl.BlockSpec((1, dv), lambda h, t, qi, kj: (0, 0)),
        ],
        out_specs=pl.BlockSpec((t_blk, dv), lambda h, t, qi, kj: (qi[t], h)),
        scratch_shapes=[
            pltpu.VMEM((2, t_blk, LANES), F32),
            pltpu.VMEM((2, t_blk, LANES), F32),
            pltpu.VMEM((2, t_blk, dv), F32),
        ],
    )
    return pl.pallas_call(
        functools.partial(_attn_prompt_body, dk=dk),
        grid_spec=grid_spec,
        out_shape=jax.ShapeDtypeStruct((s_len, n_heads * dv), BF16),
        compiler_params=_params(("parallel", "arbitrary"), VMEM_LIMIT_BYTES),
        name="diff_attn_prompt",
    )(jnp.asarray(qi), jnp.asarray(kj), q, k, v, *lam_rows, subln_w)


def _attn_sample_body(pt_ref, q_ref, *refs, n_pg, n_steps, n_heads, dv, lp, l_new):
    n_vc = dv // LANES
    k_refs, v_refs = refs[:n_pg], refs[n_pg:n_pg * (1 + n_vc)]
    kn_ref, vn_ref, lq1, lk1, lq2, lk2, sw_ref, o_ref, m_sc, l_sc, acc_sc = refs[n_pg * (1 + n_vc):]
    step = pl.program_id(1)

    @pl.when(step == 0)
    def _():
        m_sc[...] = jnp.full(m_sc.shape, -jnp.inf, F32)
        l_sc[...] = jnp.zeros(l_sc.shape, F32)
        acc_sc[...] = jnp.zeros(acc_sc.shape, F32)

    q = q_ref[...]

    def update(kb, vb, mask):
        s = _nt_dot(q, kb)
        if mask is not None:
            s = jnp.where(mask, s, -jnp.inf)
        m_new, alpha, p, l_next = _online_softmax_update(s, m_sc[...], l_sc[...])
        for h in range(n_heads):
            rows = slice(2 * h * lp, 2 * (h + 1) * lp)
            cols = slice(h * dv, (h + 1) * dv)
            acc_sc[rows, cols] = alpha[rows] * acc_sc[rows, cols] + jnp.dot(
                p[rows].astype(BF16), vb[:, cols], preferred_element_type=F32)
        m_sc[...] = m_new
        l_sc[...] = l_next

    @pl.when(step < n_steps)
    def _():
        def page_rows(refs_, groups):
            rows = refs_[0].shape[0] // groups
            return jnp.concatenate(
                [ref[pl.ds(gi, rows, stride=groups), :].astype(BF16) for gi in range(groups) for ref in refs_],
                axis=1)

        kb = jnp.concatenate([page_rows(k_refs[r:r + 1], 2 * n_heads) for r in range(n_pg)], axis=0)
        vb = jnp.concatenate([page_rows(v_refs[r * n_vc:(r + 1) * n_vc], n_heads) for r in range(n_pg)], axis=0)
        update(kb, vb, None)

    @pl.when(step == n_steps)
    def _():
        shape = (q.shape[0], kn_ref.shape[0])
        row_i = jnp.bitwise_and(lax.broadcasted_iota(jnp.int32, shape, 0), lp - 1)
        col = lax.broadcasted_iota(jnp.int32, shape, 1)
        update(kn_ref[...], vn_ref[...], jnp.logical_and(col < l_new, col <= row_i))
        lam = _lambda_value(lq1, lk1, lq2, lk2)
        for h in range(n_heads):
            r0 = 2 * h * lp
            cols = slice(h * dv, (h + 1) * dv)
            o_ref[:, cols] = _diff_finish(
                acc_sc[r0:r0 + lp, cols], l_sc[r0:r0 + lp], acc_sc[r0 + lp:r0 + 2 * lp, cols],
                l_sc[r0 + lp:r0 + 2 * lp], lam, sw_ref[...])


def _attn_sample(q_rot, k_new, v_new, cache_k, cache_v, page_table, lam_rows, subln_w, n_heads, dk, dv):
    b, l_new, w_qk = q_rot.shape
    n_maps = 2 * n_heads
    n_pages = page_table.shape[1]
    page = cache_v.shape[1] // n_heads
    w_v = n_heads * dv
    lp = max(SUBLANES, 1 << (l_new - 1).bit_length())
    assert l_new <= page
    qp = jnp.pad(q_rot.reshape(b, l_new, n_maps, dk), ((0, 0), (0, lp - l_new), (0, 0), (0, 0)))
    eye = jnp.eye(n_maps, dtype=BF16)
    q_bd = (qp.transpose(0, 2, 1, 3)[:, :, :, None, :] * eye[None, :, None, :, None]).reshape(b, n_maps * lp, w_qk)
    kn = jnp.pad(k_new, ((0, 0), (0, page - l_new), (0, 0)))
    vn = jnp.pad(v_new, ((0, 0), (0, page - l_new), (0, 0)))
    n_pg = _tile(n_pages, PAGES_PER_STEP, 1)
    n_steps = n_pages // n_pg

    assert dk == LANES and dv % LANES == 0
    n_vc = dv // LANES

    def page_map(r, cb=0):
        return lambda bi, s, pt: (pt[bi * n_pages + jnp.minimum(s, n_steps - 1) * n_pg + r], 0, cb)

    rowspec = pl.BlockSpec((1, dk), lambda bi, s, pt: (0, 0))
    seqspec = lambda rows, w: pl.BlockSpec((None, rows, w), lambda bi, s, pt: (bi, 0, 0))
    grid_spec = pltpu.PrefetchScalarGridSpec(
        num_scalar_prefetch=1,
        grid=(b, n_steps + 1),
        in_specs=(
            [seqspec(n_maps * lp, w_qk)]
            + [pl.BlockSpec((None, page * n_maps, dk), page_map(r)) for r in range(n_pg)]
            + [pl.BlockSpec((None, page * n_heads, LANES), page_map(r, cb))
               for r in range(n_pg) for cb in range(n_vc)]
            + [seqspec(page, w_qk), seqspec(page, w_v), rowspec, rowspec, rowspec, rowspec,
               pl.BlockSpec((1, dv), lambda bi, s, pt: (0, 0))]
        ),
        out_specs=seqspec(lp, w_v),
        scratch_shapes=[
            pltpu.VMEM((n_maps * lp, 1), F32),
            pltpu.VMEM((n_maps * lp, 1), F32),
            pltpu.VMEM((n_maps * lp, w_v), F32),
        ],
    )
    out = pl.pallas_call(
        functools.partial(_attn_sample_body, n_pg=n_pg, n_steps=n_steps, n_heads=n_heads, dv=dv, lp=lp,
                          l_new=l_new),
        grid_spec=grid_spec,
        out_shape=jax.ShapeDtypeStruct((b, lp, w_v), F32),
        compiler_params=_params(("parallel", "arbitrary"), VMEM_LIMIT_BYTES),
        name="diff_attn_sample",
    )(page_table.reshape(-1), q_bd, *([cache_k] * n_pg), *([cache_v] * (n_pg * n_vc)), kn, vn, *lam_rows, subln_w)
    return out[:, :l_new]


def _gla_constants(c):
    levels = c.bit_length() - 1
    assert 1 << levels == c
    idx = np.arange(c)
    i, t = idx[:, None], idx[None, :]
    sums = [t <= i, t > i]
    masks = [i == t]
    for lvl in range(1, levels + 1):
        blk, half = 1 << lvl, 1 << (lvl - 1)
        pos = idx % blk
        mid = (idx // blk) * blk + half - 1
        right, left = pos >= half, pos < half
        sums.append(right[:, None] & (t > mid[:, None]) & (t <= i))
        sums.append(left[:, None] & (t > i) & (t <= mid[:, None]))
        masks.append((idx[:, None] // blk == idx[None, :] // blk) & right[:, None] & left[None, :])
    return (jnp.asarray(np.concatenate(sums, 0).astype(np.float32), BF16),
            jnp.asarray(np.stack(masks, 0).astype(np.float32), F32))


def _gla_body(q_ref, k_ref, v_ref, r_ref, g_ref, s0_ref, sum_ref, mask_ref, gw_ref, o_ref, sout_ref, s_sc,
              *, c, levels, dk):
    ci = pl.program_id(2)

    @pl.when(ci == 0)
    def _():
        s_sc[...] = s0_ref[...].astype(F32)

    g = g_ref[...]
    g1 = g.astype(BF16)
    rem = g - g1.astype(F32)
    g2 = rem.astype(BF16)
    g3 = (rem - g2.astype(F32)).astype(BF16)
    sums = sum_ref[...]
    seg = (jnp.dot(sums, g1, preferred_element_type=F32) + jnp.dot(sums, g2, preferred_element_type=F32)
           + jnp.dot(sums, g3, preferred_element_type=F32))
    ones = jnp.ones((c, LANES), BF16)
    b_last = (_tn_dot(g1, ones) + _tn_dot(g2, ones) + _tn_dot(g3, ones))[:, 0:1]

    q = q_ref[...] * (dk ** -0.5)
    k = k_ref[...]
    v = v_ref[...].astype(BF16)
    state = s_sc[...]
    o = jnp.dot((q * jnp.exp(seg[0:c])).astype(BF16), state.astype(BF16), preferred_element_type=F32)
    a = mask_ref[0] * _nt_dot(q.astype(BF16), k.astype(BF16))
    for lvl in range(1, levels + 1):
        qd = (q * jnp.exp(seg[2 * lvl * c:(2 * lvl + 1) * c])).astype(BF16)
        kd = (k * jnp.exp(seg[(2 * lvl + 1) * c:(2 * lvl + 2) * c])).astype(BF16)
        a = a + mask_ref[lvl] * _nt_dot(qd, kd)
    o = o + jnp.dot(a.astype(BF16), v, preferred_element_type=F32)
    k_tail = (k * jnp.exp(seg[c:2 * c])).astype(BF16)
    new_state = state * jnp.exp(b_last) + _tn_dot(k_tail, v)
    s_sc[...] = new_state

    y = o * lax.rsqrt(jnp.mean(o * o, axis=-1, keepdims=True) + NORM_EPS) * gw_ref[...]
    r = r_ref[...]
    o_ref[...] = (y * (r * _sigmoid(r))).astype(o_ref.dtype)

    @pl.when(ci == pl.num_programs(2) - 1)
    def _():
        sout_ref[...] = new_state


def _gla(q, k, v, r, g, offs, s0, gla_norm_w, c):
    b, n_heads, dk, dv = s0.shape
    rows = g.shape[1]
    n_chunks = rows // c
    levels = c.bit_length() - 1
    sums, masks = _gla_constants(c)
    assert all(off % w == 0 for off, w in zip(offs, (dk, dk, dv, dv)))
    col = lambda w, off: pl.BlockSpec(
        (None, c, w), functools.partial(lambda bi, h, ci, o: (bi, ci, h + o), o=off // w))
    const3 = lambda shp: pl.BlockSpec(shp, lambda bi, h, ci: (0,) * len(shp))
    sspec = pl.BlockSpec((None, None, dk, dv), lambda bi, h, ci: (bi, h, 0, 0))
    return pl.pallas_call(
        functools.partial(_gla_body, c=c, levels=levels, dk=dk),
        grid=(b, n_heads, n_chunks),
        in_specs=[col(dk, offs[0]), col(dk, offs[1]), col(dv, offs[2]), col(dv, offs[3]), col(dk, 0), sspec,
                  const3(sums.shape), const3(masks.shape), const3((1, dv))],
        out_specs=[col(dv, 0), sspec],
        out_shape=[jax.ShapeDtypeStruct((b, rows, n_heads * dv), BF16),
                   jax.ShapeDtypeStruct((b, n_heads, dk, dv), F32)],
        scratch_shapes=[pltpu.VMEM((dk, dv), F32)],
        compiler_params=_params(("parallel", "parallel", "arbitrary"), VMEM_LIMIT_BYTES),
        name="gla",
    )(q, k, v, r, g, s0, sums, masks, gla_norm_w.reshape(1, dv).astype(F32))


def _router_body(x_ref, w_ref, wr_ref, br_ref, h_ref, idx_ref, gate_ref):
    x = x_ref[...]
    h = (x * lax.rsqrt(jnp.mean(x * x, axis=-1, keepdims=True) + NORM_EPS)) * w_ref[...]
    h_ref[...] = h
    logits = jnp.dot(h, wr_ref[...], precision=lax.Precision.HIGHEST, preferred_element_type=F32) + br_ref[...]
    lane = lax.broadcasted_iota(jnp.int32, logits.shape, 1)
    idx_out = jnp.zeros(logits.shape, jnp.int32)
    val_out = jnp.zeros(logits.shape, F32)
    top0 = None
    for kk in range(TOP_K):
        mx = jnp.max(logits, axis=-1, keepdims=True)
        idx = jnp.min(jnp.where(logits == mx, lane, LANES), axis=-1, keepdims=True)
        top0 = mx if top0 is None else top0
        idx_out = jnp.where(lane == kk, idx, idx_out)
        val_out = jnp.where(lane == kk, jnp.exp(mx - top0), val_out)
        logits = jnp.where(lane == idx, -jnp.inf, logits)
    idx_ref[...] = idx_out
    gate_ref[...] = val_out / jnp.sum(val_out, axis=-1, keepdims=True)


def _router(x, norm_w, w_router, b_router):
    n, d = x.shape
    e = w_router.shape[1]
    assert TOP_K <= e <= LANES
    wr = jnp.zeros((d, LANES), F32).at[:, :e].set(w_router.astype(F32))
    br = jnp.full((1, LANES), -jnp.inf, F32).at[0, :e].set(b_router.astype(F32))
    tr = _tile(n, ROW_TILE, SUBLANES)
    row = lambda w: pl.BlockSpec((tr, w), lambda i: (i, 0))
    fixed = lambda r, w: pl.BlockSpec((r, w), lambda i: (0, 0))
    return pl.pallas_call(
        _router_body,
        grid=(n // tr,),
        in_specs=[row(d), fixed(1, d), fixed(d, LANES), fixed(1, LANES)],
        out_specs=[row(d), row(LANES), row(LANES)],
        out_shape=[jax.ShapeDtypeStruct((n, d), F32), jax.ShapeDtypeStruct((n, LANES), jnp.int32),
                   jax.ShapeDtypeStruct((n, LANES), F32)],
        compiler_params=_params(("parallel",), VMEM_LIMIT_BYTES),
        name="ffn_norm_router",
    )(x, norm_w.reshape(1, d).astype(F32), wr, br)


def _row_gather(src_hbm, idx_ref, buf, sem, slot, n):
    def copy(src_row, r):
        return pltpu.make_async_copy(src_hbm.at[pl.ds(src_row, 1), :], buf.at[slot, pl.ds(r, 1), :], sem.at[slot])

    def issue_one(r, carry):
        copy(idx_ref[0, r], r).start()
        return carry

    def wait_one(r, carry):
        copy(0, r).wait()
        return carry

    def issue():
        lax.fori_loop(0, n, issue_one, 0, unroll=8)

    def wait():
        lax.fori_loop(0, n, wait_one, 0, unroll=8)

    return issue, wait


def _gather_body(valid_ref, tok_ref, tok_next_ref, h_hbm, o_ref, buf, sem, *, rows):
    t = pl.program_id(0)
    slot = t % 2
    issue_cur, wait_cur = _row_gather(h_hbm, tok_ref, buf, sem, slot, rows)
    issue_next, _ = _row_gather(h_hbm, tok_next_ref, buf, sem, 1 - slot, rows)

    @pl.when(jnp.logical_and(t == 0, valid_ref[0] > 0))
    def _():
        issue_cur()

    last = pl.num_programs(0) - 1

    @pl.when(jnp.logical_and(t < last, valid_ref[jnp.minimum(t + 1, last)] > 0))
    def _():
        issue_next()

    @pl.when(valid_ref[t] > 0)
    def _():
        wait_cur()
        o_ref[...] = buf[slot].astype(o_ref.dtype)

    @pl.when(valid_ref[t] == 0)
    def _():
        o_ref[...] = jnp.zeros(o_ref.shape, o_ref.dtype)


def _gather_rows(h, slot_tok, tile_valid, rows):
    n_slots = slot_tok.shape[0]
    d = h.shape[1]
    n_tiles = n_slots // rows
    grid_spec = pltpu.PrefetchScalarGridSpec(
        num_scalar_prefetch=1,
        grid=(n_tiles,),
        in_specs=[
            pl.BlockSpec((None, 1, rows), lambda t, valid: (t, 0, 0), memory_space=pltpu.SMEM),
            pl.BlockSpec((None, 1, rows), lambda t, valid: (jnp.minimum(t + 1, n_tiles - 1), 0, 0),
                         memory_space=pltpu.SMEM),
            pl.BlockSpec(memory_space=pl.ANY),
        ],
        out_specs=pl.BlockSpec((rows, d), lambda t, valid: (t, 0)),
        scratch_shapes=[pltpu.VMEM((2, rows, d), F32), pltpu.SemaphoreType.DMA((2,))],
    )
    return pl.pallas_call(
        functools.partial(_gather_body, rows=rows),
        grid_spec=grid_spec,
        out_shape=jax.ShapeDtypeStruct((n_slots, d), BF16),
        compiler_params=_params(("arbitrary",), VMEM_LIMIT_BYTES),
        name="moe_gather",
    )(tile_valid, slot_tok.reshape(n_tiles, 1, rows), slot_tok.reshape(n_tiles, 1, rows), h)


def _expert_rows(n_rows, x_ref, w_refs, w_sc, o_ref, tm, finish):
    k = w_sc.shape[0]
    kc = _tile(k, MOE_CAST_CHUNK, LANES)
    acc = None
    for c in range(k // kc):
        ks = slice(c * kc, (c + 1) * kc)
        w_c = jnp.concatenate([w[ks, :].astype(BF16) for w in w_refs], axis=1)
        w_sc[ks, :] = w_c
        part = jnp.dot(x_ref[0:tm, ks], w_c, preferred_element_type=F32)
        acc = part if acc is None else acc + part
    o_ref[0:tm, :] = finish(acc).astype(o_ref.dtype)

    def tile(start, size):
        rs = pl.ds(pl.multiple_of(start, tm), size)
        o_ref[rs, :] = finish(jnp.dot(x_ref[rs, :], w_sc[...], preferred_element_type=F32)).astype(o_ref.dtype)

    n_small = (n_rows + tm - 1) // tm
    rest = n_small - 1
    n_quad = rest // 4

    def quad_tile(r, carry):
        tile(tm + r * (4 * tm), 4 * tm)
        return carry

    def zero(r, carry):
        o_ref[pl.ds(pl.multiple_of(r * tm, tm), tm), :] = jnp.zeros((tm, o_ref.shape[1]), o_ref.dtype)
        return carry

    lax.fori_loop(0, n_quad, quad_tile, 0)
    base = tm + n_quad * (4 * tm)

    @pl.when(rest % 4 >= 2)
    def _():
        tile(base, 2 * tm)

    @pl.when(rest % 2 == 1)
    def _():
        tile(base + (rest % 4 // 2) * (2 * tm), tm)

    lax.fori_loop(n_small, o_ref.shape[0] // tm, zero, 0)


def _expert_up_body(ge_ref, gr_ref, nv_ref, x_ref, wg_ref, wu_ref, bg_ref, bu_ref, o_ref, w_sc, *, tm):
    g = pl.program_id(0)
    tn = o_ref.shape[1]

    @pl.when(g < nv_ref[0])
    def _():
        bias = jnp.concatenate([bg_ref[...], bu_ref[...]], axis=1)

        def swiglu(gu):
            gu = gu + bias
            gt = jnp.minimum(gu[:, :tn], SWIGLU_LIMIT)
            up = jnp.clip(gu[:, tn:], -SWIGLU_LIMIT, SWIGLU_LIMIT)
            return (up + 1.0) * (gt * _sigmoid(SWIGLU_ALPHA * gt))

        _expert_rows(gr_ref[g], x_ref, (wg_ref, wu_ref), w_sc, o_ref, tm, swiglu)


def _expert_down_body(ge_ref, gr_ref, nv_ref, x_ref, w_ref, b_ref, o_ref, w_sc, *, tm):
    g = pl.program_id(0)

    @pl.when(g < nv_ref[0])
    def _():
        _expert_rows(gr_ref[g], x_ref, (w_ref,), w_sc, o_ref, tm, lambda y: y + b_ref[...])


def _expert_ffn(xs, group_expert, group_rows, n_valid, w_gate_up, b_gate_up, w_down, b_down, tm):
    n_groups, g_rows, d = xs.shape
    e, _, two_de = w_gate_up.shape
    de = two_de // 2
    tn_u = _tile(de, MOE_TN, LANES)
    tn_d = _tile(d, MOE_TN_DOWN, LANES)
    nj_u, nj_d = de // tn_u, d // tn_d

    def maps(nj):
        gv = lambda g, nv: jnp.minimum(g, nv[0] - 1)
        jv = lambda g, j, nv: jnp.where(g < nv[0], j, nj - 1)
        return gv, jv

    gv, jv = maps(nj_u)
    act = pl.pallas_call(
        functools.partial(_expert_up_body, tm=tm),
        grid_spec=pltpu.PrefetchScalarGridSpec(
            num_scalar_prefetch=3,
            grid=(n_groups, nj_u),
            in_specs=[
                pl.BlockSpec((None, g_rows, d), lambda g, j, ge, gr, nv: (gv(g, nv), 0, 0)),
                pl.BlockSpec((None, d, tn_u), lambda g, j, ge, gr, nv: (ge[gv(g, nv)], 0, jv(g, j, nv))),
                pl.BlockSpec((None, d, tn_u), lambda g, j, ge, gr, nv: (ge[gv(g, nv)], 0, nj_u + jv(g, j, nv))),
                pl.BlockSpec((None, 1, tn_u), lambda g, j, ge, gr, nv: (ge[gv(g, nv)], 0, jv(g, j, nv))),
                pl.BlockSpec((None, 1, tn_u), lambda g, j, ge, gr, nv: (ge[gv(g, nv)], 0, nj_u + jv(g, j, nv))),
            ],
            out_specs=pl.BlockSpec((None, g_rows, tn_u), lambda g, j, ge, gr, nv: (gv(g, nv), 0, jv(g, j, nv))),
            scratch_shapes=[pltpu.VMEM((d, 2 * tn_u), BF16)],
        ),
        out_shape=jax.ShapeDtypeStruct((n_groups, g_rows, de), BF16),
        compiler_params=_params(("arbitrary", "arbitrary"), VMEM_LIMIT_BYTES),
        name="moe_gate_up",
    )(group_expert, group_rows, n_valid, xs, w_gate_up, w_gate_up, b_gate_up.reshape(e, 1, two_de),
      b_gate_up.reshape(e, 1, two_de))

    gv, jv = maps(nj_d)
    return pl.pallas_call(
        functools.partial(_expert_down_body, tm=tm),
        grid_spec=pltpu.PrefetchScalarGridSpec(
            num_scalar_prefetch=3,
            grid=(n_groups, nj_d),
            in_specs=[
                pl.BlockSpec((None, g_rows, de), lambda g, j, ge, gr, nv: (gv(g, nv), 0, 0)),
                pl.BlockSpec((None, de, tn_d), lambda g, j, ge, gr, nv: (ge[gv(g, nv)], 0, jv(g, j, nv))),
                pl.BlockSpec((None, 1, tn_d), lambda g, j, ge, gr, nv: (ge[gv(g, nv)], 0, jv(g, j, nv))),
            ],
            out_specs=pl.BlockSpec((None, g_rows, tn_d), lambda g, j, ge, gr, nv: (gv(g, nv), 0, jv(g, j, nv))),
            scratch_shapes=[pltpu.VMEM((de, tn_d), BF16)],
        ),
        out_shape=jax.ShapeDtypeStruct((n_groups, g_rows, d), F32),
        compiler_params=_params(("arbitrary", "arbitrary"), VMEM_LIMIT_BYTES),
        name="moe_down",
    )(group_expert, group_rows, n_valid, act, w_down, b_down.reshape(e, 1, d))


def _combine_body(pos_ref, pos_next_ref, y_hbm, gate_ref, x_ref, w_ref, o_ref, buf, sem, *, rows):
    t = pl.program_id(0)
    slot = t % 2
    issue_cur, wait_cur = _row_gather(y_hbm, pos_ref, buf, sem, slot, TOP_K * rows)
    issue_next, _ = _row_gather(y_hbm, pos_next_ref, buf, sem, 1 - slot, TOP_K * rows)

    @pl.when(t == 0)
    def _():
        issue_cur()

    @pl.when(t < pl.num_programs(0) - 1)
    def _():
        issue_next()

    wait_cur()
    gates = gate_ref[...]
    x = x_ref[...]
    for kk in range(TOP_K):
        x = x + gates[:, kk:kk + 1] * buf[slot, kk * rows:(kk + 1) * rows, :]
    y = x * lax.rsqrt(jnp.mean(x * x, axis=-1, keepdims=True) + NORM_EPS)
    o_ref[...] = y * w_ref[...]


def _combine(pos_tiles, y_slots, gates, x1, norm_w, row0, n_rows, rows):
    d = x1.shape[1]
    assert row0 % rows == 0 and n_rows % rows == 0
    rb = row0 // rows
    n_tiles = n_rows // rows
    grid_spec = pltpu.PrefetchScalarGridSpec(
        num_scalar_prefetch=0,
        grid=(n_tiles,),
        in_specs=[
            pl.BlockSpec((None, 1, TOP_K * rows), lambda t: (t + rb, 0, 0), memory_space=pltpu.SMEM),
            pl.BlockSpec((None, 1, TOP_K * rows), lambda t: (jnp.minimum(t + 1, n_tiles - 1) + rb, 0, 0),
                         memory_space=pltpu.SMEM),
            pl.BlockSpec(memory_space=pl.ANY),
            pl.BlockSpec((rows, LANES), lambda t: (t + rb, 0)),
            pl.BlockSpec((rows, d), lambda t: (t + rb, 0)),
            pl.BlockSpec((1, d), lambda t: (0, 0)),
        ],
        out_specs=pl.BlockSpec((rows, d), lambda t: (t, 0)),
        scratch_shapes=[pltpu.VMEM((2, TOP_K * rows, d), F32), pltpu.SemaphoreType.DMA((2,))],
    )
    return pl.pallas_call(
        functools.partial(_combine_body, rows=rows),
        grid_spec=grid_spec,
        out_shape=jax.ShapeDtypeStruct((n_rows, d), F32),
        compiler_params=_params(("arbitrary",), VMEM_LIMIT_BYTES),
        name="moe_combine_norm",
    )(pos_tiles, pos_tiles, y_slots, gates, x1, norm_w.reshape(1, d).astype(F32))


def _routing_tables(expert_ids, n_experts, g_rows, n_groups, gather_rows):
    n = expert_ids.shape[0]
    m = n * TOP_K
    e_flat = expert_ids.reshape(-1)
    onehot = (e_flat[:, None] == jnp.arange(n_experts, dtype=jnp.int32)[None, :]).astype(jnp.int32)
    before = jnp.cumsum(onehot, axis=0) - onehot
    rank = jnp.take_along_axis(before, e_flat[:, None], axis=1)[:, 0]
    counts = jnp.sum(onehot, axis=0)
    groups_per_e = (counts + g_rows - 1) // g_rows
    g_end = jnp.cumsum(groups_per_e)
    g_base = g_end - groups_per_e
    slot = (g_base[e_flat] + rank // g_rows) * g_rows + rank % g_rows
    n_valid = g_end[-1]
    gid = jnp.arange(n_groups, dtype=jnp.int32)
    g_exp = jnp.minimum(jnp.searchsorted(g_end, gid, side="right"), n_experts - 1).astype(jnp.int32)
    g_cnt = jnp.clip(counts[g_exp] - (gid - g_base[g_exp]) * g_rows, 0, g_rows)
    g_cnt = jnp.where(gid < n_valid, g_cnt, 0).astype(jnp.int32)
    slot_tok = jnp.zeros((n_groups * g_rows,), jnp.int32).at[slot].set(jnp.arange(m, dtype=jnp.int32) // TOP_K)
    tiles_per_g = g_rows // gather_rows
    tile_start = (jnp.arange(n_groups * tiles_per_g, dtype=jnp.int32) % tiles_per_g) * gather_rows
    tile_valid = (tile_start < jnp.repeat(g_cnt, tiles_per_g)).astype(jnp.int32)
    return slot.astype(jnp.int32), slot_tok, tile_valid, g_exp, g_cnt, n_valid.astype(jnp.int32).reshape(1)


def kernel(x_prompt, x_sample, cache_k, cache_v, state_gla, page_table, norm_mix_w, w_in, lambda_q1, lambda_k1, lambda_q2, lambda_k2, subln_w, w_alpha2, b_alpha, gla_norm_w, w_branch_a, w_branch_b, w_out, norm_ffn_w, w_router, b_router, w_gate_up, b_gate_up, w_down, b_down, norm_final_w):
    bp, s_len, d = x_prompt.shape
    bs, l_new, _ = x_sample.shape
    depth, n_pool, page, n_maps, dk_a = cache_k.shape
    _, _, _, h_a, dv_a = cache_v.shape
    _, _, h_b, dk_b, dv_b = state_gla.shape
    rank = w_alpha2.shape[1]
    n_experts = w_router.shape[2]
    assert depth == 1 and bp == 1 and n_maps == 2 * h_a
    n_p, n_s = bp * s_len, bs * l_new
    n_all = n_p + n_s
    sizes = (n_maps * dk_a, n_maps * dk_a, h_a * dv_a, h_b * dk_b, h_b * dk_b, h_b * dv_b, h_b * dv_b, d, d, rank)
    offs = [int(o) for o in np.cumsum((0,) + sizes)]
    n_main = offs[9]
    assert w_in.shape[2] == offs[10]
    past = page_table.shape[1] * page

    x_all = jnp.concatenate([x_prompt.reshape(n_p, d), x_sample.reshape(n_s, d)], axis=0)
    u = _rmsnorm(x_all, norm_mix_w[0], BF16)
    w_in2 = w_in.reshape(d, offs[10])
    z = _matmul([(u, w_in2, 0)], [], n_main, _epi_plain, F32, "in_proj", MM_TM_WIDE)
    g_all = _decay(u, w_in2[:, n_main:], w_alpha2[0], b_alpha[0])

    lam_rows = [p[0].reshape(1, dk_a).astype(F32) for p in (lambda_q1, lambda_k1, lambda_q2, lambda_k2)]
    sw = subln_w[0].reshape(1, dv_a).astype(F32)

    pos_p = jnp.arange(s_len, dtype=jnp.int32)
    q_p, kf_p, kb_p, vf_p, vb_p = _prep(z, 0, n_p, pos_p, n_maps, dk_a, h_a * dv_a, offs[0:3])
    oa_p = _attn_prompt(q_p, kb_p, vb_p, lam_rows, sw, h_a, dk_a, dv_a)
    z3 = z.reshape(1, n_all, n_main)
    c_p = _tile(s_len, GLA_CHUNK, SUBLANES)
    ob_p, st_p = _gla(z3, z3, z3, z3, g_all.reshape(1, n_all, h_b * dk_b)[:, :n_p], offs[3:7],
                      jnp.zeros((bp, h_b, dk_b, dv_b), F32), gla_norm_w[0], c_p)

    pos_s = jnp.tile(past + jnp.arange(l_new, dtype=jnp.int32), bs)
    q_s, kf_s, kb_s, vf_s, vb_s = _prep(z, n_p, n_s, pos_s, n_maps, dk_a, h_a * dv_a, offs[0:3])
    oa_s = _attn_sample(
        q_s.reshape(bs, l_new, -1), kb_s.reshape(bs, l_new, -1), vb_s.reshape(bs, l_new, -1),
        cache_k.reshape(n_pool, page * n_maps, dk_a), cache_v.reshape(n_pool, page * h_a, dv_a),
        page_table, lam_rows, sw, h_a, dk_a, dv_a)
    c_s = max(SUBLANES, 1 << (l_new - 1).bit_length())
    pad_s = lambda a: jnp.pad(a.reshape(bs, l_new, -1), ((0, 0), (0, c_s - l_new), (0, 0)))
    zs = pad_s(z[n_p:, offs[3]:offs[7]])
    o3 = offs[3]
    ob_s, st_s = _gla(zs, zs, zs, zs, pad_s(g_all[n_p:]), [o - o3 for o in offs[3:7]],
                      state_gla[0].astype(F32), gla_norm_w[0], c_s)

    oa = jnp.concatenate([oa_p, oa_s.reshape(n_s, -1).astype(BF16)], axis=0)
    ob = jnp.concatenate([ob_p.reshape(n_p, -1), ob_s[:, :l_new].reshape(n_s, -1)], axis=0)
    mix = _matmul([(oa, w_branch_a[0].astype(BF16), 0), (ob, w_branch_b[0].astype(BF16), 0)],
                  [(z, offs[7]), (z, offs[8])], d, _epi_gated_merge, BF16, "branch_merge")
    x1 = _matmul([(mix, w_out[0].astype(BF16), 0)], [(x_all, 0)], d, _epi_residual, F32, "out_proj")

    h, top_i, gates = _router(x1, norm_ffn_w[0], w_router[0], b_router[0])
    m = n_all * TOP_K
    g_rows = -(-int(math.ceil(MOE_GROUP_SLACK * m / n_experts)) // MOE_TM) * MOE_TM
    gather_rows = _tile(g_rows, GATHER_ROWS, BF16_SUBLANES)
    n_groups = -(-m // g_rows) + n_experts
    slot, slot_tok, tile_valid, g_exp, g_cnt, n_valid = _routing_tables(
        top_i[:, :TOP_K], n_experts, g_rows, n_groups, gather_rows)
    xs = _gather_rows(h, slot_tok, tile_valid, gather_rows)
    y_slots = _expert_ffn(xs.reshape(n_groups, g_rows, d), g_exp, g_cnt, n_valid,
                          w_gate_up[0], b_gate_up[0], w_down[0], b_down[0], MOE_TM)
    y_slots = y_slots.reshape(n_groups * g_rows, d)
    rows_c = _tile(math.gcd(n_p, n_s), COMBINE_ROWS, SUBLANES)
    pos_tiles = slot.reshape(n_all // rows_c, rows_c, TOP_K).transpose(0, 2, 1).reshape(n_all // rows_c, 1, TOP_K * rows_c)
    y_p = _combine(pos_tiles, y_slots, gates, x1, norm_final_w, 0, n_p, rows_c)
    y_s = _combine(pos_tiles, y_slots, gates, x1, norm_final_w, n_p, n_s, rows_c)

    return (
        y_p.reshape(bp, s_len, d),
        y_s.reshape(bs, l_new, d),
        kf_p.reshape(1, bp, s_len, n_maps, dk_a),
        vf_p.reshape(1, bp, s_len, h_a, dv_a),
        st_p.reshape(1, bp, h_b, dk_b, dv_b),
        kf_s.reshape(1, bs, l_new, n_maps, dk_a),
        vf_s.reshape(1, bs, l_new, h_a, dv_a),
        st_s.reshape(1, bs, h_b, dk_b, dv_b),
    )
```

```python
import functools
import math

import jax
import jax.numpy as jnp
import numpy as np
from jax import lax
from jax.experimental import pallas as pl
from jax.experimental.pallas import tpu as pltpu

F32 = jnp.float32
BF16 = jnp.bfloat16

ROPE_THETA = 500000.0
GLA_TAU = 16.0
TOP_K = 4
SWIGLU_LIMIT = 7.0
SWIGLU_ALPHA = 1.702
NORM_EPS = 1e-5
LAMBDA_INIT = 0.8 - 0.6 * math.exp(-0.3 * 0)

LANES = 128
SUBLANES = 8
BF16_SUBLANES = 16
VMEM_LIMIT_BYTES = 56 * 1024 * 1024

ROW_TILE = 256
MM_TM = 1024
MM_TM_WIDE = 2048
MM_TN = 512
ATTN_TILE = 1024
ATTN_ROW_BLOCK = 128
PAGES_PER_STEP = 8
GLA_CHUNK = 128
MOE_TM = 128
MOE_TN = 256
MOE_TN_DOWN = 512
MOE_CAST_CHUNK = 1024
MOE_GROUP_SLACK = 1.45
GATHER_ROWS = 128
COMBINE_ROWS = 64


def _tile(n, pref, mult):
    best = None
    for d in range(mult, min(n, pref) + 1, mult):
        if n % d == 0:
            best = d
    return best if best is not None else n


def _params(sem, vmem=None):
    return pltpu.CompilerParams(dimension_semantics=sem, vmem_limit_bytes=vmem)


def _sigmoid(x):
    return 1.0 / (1.0 + jnp.exp(-x))


def _nt_dot(a, b):
    return lax.dot_general(a, b, (((1,), (1,)), ((), ())), preferred_element_type=F32)


def _tn_dot(a, b):
    return lax.dot_general(a, b, (((0,), (0,)), ((), ())), preferred_element_type=F32)


def _rmsnorm_body(x_ref, w_ref, o_ref):
    x = x_ref[...].astype(F32)
    y = x * lax.rsqrt(jnp.mean(x * x, axis=-1, keepdims=True) + NORM_EPS)
    o_ref[...] = (y * w_ref[...]).astype(o_ref.dtype)


def _rmsnorm(x, w, out_dtype):
    n, d = x.shape
    tr = _tile(n, ROW_TILE, BF16_SUBLANES)
    return pl.pallas_call(
        _rmsnorm_body,
        grid=(n // tr,),
        in_specs=[pl.BlockSpec((tr, d), lambda i: (i, 0)), pl.BlockSpec((1, d), lambda i: (0, 0))],
        out_specs=pl.BlockSpec((tr, d), lambda i: (i, 0)),
        out_shape=jax.ShapeDtypeStruct((n, d), out_dtype),
        compiler_params=_params(("parallel",)),
        name="rmsnorm",
    )(x, w.reshape(1, d).astype(F32))


def _mm_body(*refs, n_pairs, epilogue):
    o_ref = refs[-1]
    accs = [
        jnp.dot(refs[2 * p][...].astype(BF16), refs[2 * p + 1][...].astype(BF16), preferred_element_type=F32)
        for p in range(n_pairs)
    ]
    extras = [r[...] for r in refs[2 * n_pairs:-1]]
    o_ref[...] = epilogue(accs, extras).astype(o_ref.dtype)


def _matmul(pairs, extras, n_out, epilogue, out_dtype, name, tm_pref=MM_TM):
    m = pairs[0][0].shape[0]
    tm = _tile(m, tm_pref, BF16_SUBLANES)
    tn = _tile(n_out, MM_TN, LANES)
    in_specs, args = [], []
    for a, b, off in pairs:
        k = a.shape[1]
        assert off % tn == 0 and b.shape[-2] == k
        in_specs.append(pl.BlockSpec((tm, k), lambda i, j: (i, 0)))
        if b.ndim == 3:
            assert b.shape[0] == 1
            in_specs.append(pl.BlockSpec((None, k, tn), functools.partial(lambda i, j, o: (0, 0, j + o), o=off // tn)))
        else:
            in_specs.append(pl.BlockSpec((k, tn), functools.partial(lambda i, j, o: (0, j + o), o=off // tn)))
        args += [a, b]
    for e, off in extras:
        assert off % tn == 0
        in_specs.append(pl.BlockSpec((tm, tn), functools.partial(lambda i, j, o: (i, j + o), o=off // tn)))
        args.append(e)
    return pl.pallas_call(
        functools.partial(_mm_body, n_pairs=len(pairs), epilogue=epilogue),
        grid=(m // tm, n_out // tn),
        in_specs=in_specs,
        out_specs=pl.BlockSpec((tm, tn), lambda i, j: (i, j)),
        out_shape=jax.ShapeDtypeStruct((m, n_out), out_dtype),
        compiler_params=_params(("parallel", "arbitrary"), VMEM_LIMIT_BYTES),
        name=name,
    )(*args)


def _epi_plain(accs, extras):
    return accs[0]


def _epi_gated_merge(accs, extras):
    return _sigmoid(extras[0]) * accs[0] + _sigmoid(extras[1]) * accs[1]


def _epi_residual(accs, extras):
    return extras[0] + accs[0]


def _decay_body(u_ref, wlr_ref, wa2_ref, ba_ref, g_ref):
    a = jnp.dot(u_ref[...], wlr_ref[...], preferred_element_type=F32)
    x = jnp.dot(a.astype(BF16), wa2_ref[...], preferred_element_type=F32) + ba_ref[...]
    g_ref[...] = (jnp.minimum(x, 0.0) - jnp.log1p(jnp.exp(-jnp.abs(x)))) * (1.0 / GLA_TAU)


def _decay(u, w_lr, w_alpha2, b_alpha):
    n, d = u.shape
    rank, c = w_alpha2.shape
    assert rank <= LANES
    wlr = jnp.zeros((d, LANES), BF16).at[:, :rank].set(w_lr.astype(BF16))
    wa2 = jnp.zeros((LANES, c), BF16).at[:rank].set(w_alpha2.astype(BF16))
    tr = _tile(n, 2 * ROW_TILE, BF16_SUBLANES)
    return pl.pallas_call(
        _decay_body,
        grid=(n // tr,),
        in_specs=[
            pl.BlockSpec((tr, d), lambda i: (i, 0)),
            pl.BlockSpec((d, LANES), lambda i: (0, 0)),
            pl.BlockSpec((LANES, c), lambda i: (0, 0)),
            pl.BlockSpec((1, c), lambda i: (0, 0)),
        ],
        out_specs=pl.BlockSpec((tr, c), lambda i: (i, 0)),
        out_shape=jax.ShapeDtypeStruct((n, c), F32),
        compiler_params=_params(("parallel",)),
        name="gla_decay",
    )(u, wlr, wa2, b_alpha.reshape(1, c).astype(F32))


def _rope_tables(pos, dk):
    rot = dk // 4
    half = rot // 2
    inv_freq = jnp.power(jnp.float32(ROPE_THETA), -jnp.arange(half, dtype=F32) * (2.0 / rot))
    ang = pos.astype(F32)[:, None] * inv_freq[None, :]
    cos, sin = jnp.cos(ang), jnp.sin(ang)
    n = pos.shape[0]
    pad = jnp.zeros((n, dk - rot), F32)
    zero = jnp.zeros((n, half), F32)
    c = jnp.concatenate([cos, cos, pad + 1.0], axis=1)
    s1 = jnp.concatenate([-sin, zero, pad], axis=1)
    s2 = jnp.concatenate([zero, sin, pad], axis=1)
    return c, s1, s2


def _prep_body(zq_ref, zk_ref, zv_ref, c_ref, s1_ref, s2_ref, q_o, kf_o, kb_o, vf_o, vb_o, *, n_maps, dk):
    half = dk // 8
    c, s1, s2 = c_ref[...], s1_ref[...], s2_ref[...]
    q_scale = (dk ** -0.5) * math.log2(math.e)

    def rope(x):
        return x * c + pltpu.roll(x, dk - half, 1) * s1 + pltpu.roll(x, half, 1) * s2

    for m in range(n_maps):
        sl = slice(m * dk, (m + 1) * dk)
        q_o[:, sl] = (rope(zq_ref[:, sl]) * q_scale).astype(q_o.dtype)
        k = rope(zk_ref[:, sl])
        kf_o[:, sl] = k
        kb_o[:, sl] = k.astype(kb_o.dtype)
    v = zv_ref[...]
    vf_o[...] = v
    vb_o[...] = v.astype(vb_o.dtype)


def _prep(z, row0, n_rows, pos, n_maps, dk, w_v, offs):
    w_qk = n_maps * dk
    tr = _tile(n_rows, ROW_TILE, BF16_SUBLANES)
    assert row0 % tr == 0 and offs[0] % w_qk == 0 and offs[1] % w_qk == 0 and offs[2] % w_v == 0
    rb = row0 // tr
    c, s1, s2 = _rope_tables(pos, dk)
    zspec = lambda w, off: pl.BlockSpec((tr, w), functools.partial(lambda i, o: (i + rb, o), o=off // w))
    tspec = pl.BlockSpec((tr, dk), lambda i: (i, 0))
    ospec = lambda w: pl.BlockSpec((tr, w), lambda i: (i, 0))
    sds = lambda w, dt: jax.ShapeDtypeStruct((n_rows, w), dt)
    return pl.pallas_call(
        functools.partial(_prep_body, n_maps=n_maps, dk=dk),
        grid=(n_rows // tr,),
        in_specs=[zspec(w_qk, offs[0]), zspec(w_qk, offs[1]), zspec(w_v, offs[2]), tspec, tspec, tspec],
        out_specs=[ospec(w_qk), ospec(w_qk), ospec(w_qk), ospec(w_v), ospec(w_v)],
        out_shape=[sds(w_qk, BF16), sds(w_qk, F32), sds(w_qk, BF16), sds(w_v, F32), sds(w_v, BF16)],
        compiler_params=_params(("parallel",), VMEM_LIMIT_BYTES),
        name="qkv_rope",
    )(z, z, z, c, s1, s2)


def _lambda_value(lq1, lk1, lq2, lk2):
    a = jnp.exp(jnp.sum(lq1[...] * lk1[...], axis=-1, keepdims=True))
    b = jnp.exp(jnp.sum(lq2[...] * lk2[...], axis=-1, keepdims=True))
    return a - b + LAMBDA_INIT


def _diff_finish(a1, l1, a2, l2, lam, sw):
    o = a1 / l1 - lam * (a2 / l2)
    y = o * lax.rsqrt(jnp.mean(o * o, axis=-1, keepdims=True) + NORM_EPS)
    return (y * sw) * (1.0 - LAMBDA_INIT)


def _online_softmax_update(s, m_prev, l_prev):
    m_new = jnp.maximum(m_prev, jnp.max(s, axis=-1, keepdims=True))
    alpha = jnp.exp2(m_prev - m_new)
    p = jnp.exp2(s - m_new)
    return m_new, alpha, p, alpha * l_prev + jnp.sum(p, axis=-1, keepdims=True)


def _attn_prompt_body(qi_ref, kj_ref, q_ref, k_ref, v_ref, lq1, lk1, lq2, lk2, sw_ref, o_ref,
                      m_sc, l_sc, acc_sc, *, dk):
    t = pl.program_id(1)
    i = qi_ref[t]
    j = kj_ref[t]

    @pl.when(j == 0)
    def _():
        m_sc[...] = jnp.full(m_sc.shape, -jnp.inf, F32)
        l_sc[...] = jnp.zeros(l_sc.shape, F32)
        acc_sc[...] = jnp.zeros(acc_sc.shape, F32)

    t_blk, dv = acc_sc.shape[1], acc_sc.shape[2]
    rb = min(t_blk, ATTN_ROW_BLOCK)

    def lanes(x, width):
        return x if width == LANES else jnp.concatenate([x] * (width // LANES), axis=1)

    def step(masked):
        for c in range(2):
            for r in range(t_blk // rb):
                rows = slice(r * rb, (r + 1) * rb)
                n_k = min(t_blk, -(-(r + 1) * rb // LANES) * LANES) if masked else t_blk
                s = _nt_dot(q_ref[rows, c * dk:(c + 1) * dk], k_ref[0:n_k, c * dk:(c + 1) * dk])
                if masked:
                    row = lax.broadcasted_iota(jnp.int32, s.shape, 0) + r * rb
                    col = lax.broadcasted_iota(jnp.int32, s.shape, 1)
                    s = jnp.where(col <= row, s, -jnp.inf)
                m_prev = m_sc[c, rows, :]
                m_new = jnp.maximum(m_prev, jnp.max(s, axis=-1, keepdims=True))
                alpha = jnp.exp2(m_prev - m_new)
                p = jnp.exp2(s - lanes(m_new, n_k))
                l_sc[c, rows, :] = alpha * l_sc[c, rows, :] + jnp.sum(p, axis=-1, keepdims=True)
                acc_sc[c, rows, :] = lanes(alpha, dv) * acc_sc[c, rows, :] + jnp.dot(
                    p.astype(BF16), v_ref[0:n_k, :], preferred_element_type=F32)
                m_sc[c, rows, :] = m_new

    @pl.when(j < i)
    def _():
        step(False)

    @pl.when(j == i)
    def _():
        step(True)
        lam = _lambda_value(lq1, lk1, lq2, lk2)
        o_ref[...] = _diff_finish(acc_sc[0], l_sc[0][:, 0:1], acc_sc[1], l_sc[1][:, 0:1], lam,
                                  sw_ref[...]).astype(o_ref.dtype)


def _attn_prompt(q, k, v, lam_rows, subln_w, n_heads, dk, dv):
    s_len = q.shape[0]
    t_blk = _tile(s_len, ATTN_TILE, LANES)
    nb = s_len // t_blk
    qi = np.concatenate([np.full(i + 1, i, np.int32) for i in range(nb)])
    kj = np.concatenate([np.arange(i + 1, dtype=np.int32) for i in range(nb)])
    rowspec = pl.BlockSpec((1, dk), lambda h, t, qi, kj: (0, 0))
    grid_spec = pltpu.PrefetchScalarGridSpec(
        num_scalar_prefetch=2,
        grid=(n_heads, len(qi)),
        in_specs=[
            pl.BlockSpec((t_blk, 2 * dk), lambda h, t, qi, kj: (qi[t], h)),
            pl.BlockSpec((t_blk, 2 * dk), lambda h, t, qi, kj: (kj[t], h)),
            pl.BlockSpec((t_blk, dv), lambda h, t, qi, kj: (kj[t], h)),
            rowspec, rowspec, rowspec, rowspec,
            pl.BlockSpec((1, dv), lambda h, t, qi, kj: (0, 0)),
        ],
        out_specs=pl.BlockSpec((t_blk, dv), lambda h, t, qi, kj: (qi[t], h)),
        scratch_shapes=[
            pltpu.VMEM((2, t_blk, LANES), F32),
            pltpu.VMEM((2, t_blk, LANES), F32),
            pltpu.VMEM((2, t_blk, dv), F32),
        ],
    )
    return pl.pallas_call(
        functools.partial(_attn_prompt_body, dk=dk),
        grid_spec=grid_spec,
        out_shape=jax.ShapeDtypeStruct((s_len, n_heads * dv), BF16),
        compiler_params=_params(("parallel", "arbitrary"), VMEM_LIMIT_BYTES),
        name="diff_attn_prompt",
    )(jnp.asarray(qi), jnp.asarray(kj), q, k, v, *lam_rows, subln_w)


def _attn_sample_body(pt_ref, q_ref, *refs, n_pg, n_steps, n_heads, dv, lp, l_new):
    n_vc = dv // LANES
    k_refs, v_refs = refs[:n_pg], refs[n_pg:n_pg * (1 + n_vc)]
    kn_ref, vn_ref, lq1, lk1, lq2, lk2, sw_ref, o_ref, m_sc, l_sc, acc_sc = refs[n_pg * (1 + n_vc):]
    step = pl.program_id(1)

    @pl.when(step == 0)
    def _():
        m_sc[...] = jnp.full(m_sc.shape, -jnp.inf, F32)
        l_sc[...] = jnp.zeros(l_sc.shape, F32)
        acc_sc[...] = jnp.zeros(acc_sc.shape, F32)

    q = q_ref[...]

    def update(kb, vb, mask):
        s = _nt_dot(q, kb)
        if mask is not None:
            s = jnp.where(mask, s, -jnp.inf)
        m_new, alpha, p, l_next = _online_softmax_update(s, m_sc[...], l_sc[...])
        for h in range(n_heads):
            rows = slice(2 * h * lp, 2 * (h + 1) * lp)
            cols = slice(h * dv, (h + 1) * dv)
            acc_sc[rows, cols] = alpha[rows] * acc_sc[rows, cols] + jnp.dot(
                p[rows].astype(BF16), vb[:, cols], preferred_element_type=F32)
        m_sc[...] = m_new
        l_sc[...] = l_next

    @pl.when(step < n_steps)
    def _():
        def page_rows(refs_, groups):
            rows = refs_[0].shape[0] // groups
            return jnp.concatenate(
                [ref[pl.ds(gi, rows, stride=groups), :].astype(BF16) for gi in range(groups) for ref in refs_],
                axis=1)

        kb = jnp.concatenate([page_rows(k_refs[r:r + 1], 2 * n_heads) for r in range(n_pg)], axis=0)
        vb = jnp.concatenate([page_rows(v_refs[r * n_vc:(r + 1) * n_vc], n_heads) for r in range(n_pg)], axis=0)
        update(kb, vb, None)

    @pl.when(step == n_steps)
    def _():
        shape = (q.shape[0], kn_ref.shape[0])
        row_i = jnp.bitwise_and(lax.broadcasted_iota(jnp.int32, shape, 0), lp - 1)
        col = lax.broadcasted_iota(jnp.int32, shape, 1)
        update(kn_ref[...], vn_ref[...], jnp.logical_and(col < l_new, col <= row_i))
        lam = _lambda_value(lq1, lk1, lq2, lk2)
        for h in range(n_heads):
            r0 = 2 * h * lp
            cols = slice(h * dv, (h + 1) * dv)
            o_ref[:, cols] = _diff_finish(
                acc_sc[r0:r0 + lp, cols], l_sc[r0:r0 + lp], acc_sc[r0 + lp:r0 + 2 * lp, cols],
                l_sc[r0 + lp:r0 + 2 * lp], lam, sw_ref[...])


def _attn_sample(q_rot, k_new, v_new, cache_k, cache_v, page_table, lam_rows, subln_w, n_heads, dk, dv):
    b, l_new, w_qk = q_rot.shape
    n_maps = 2 * n_heads
    n_pages = page_table.shape[1]
    page = cache_v.shape[1] // n_heads
    w_v = n_heads * dv
    lp = max(SUBLANES, 1 << (l_new - 1).bit_length())
    assert l_new <= page
    qp = jnp.pad(q_rot.reshape(b, l_new, n_maps, dk), ((0, 0), (0, lp - l_new), (0, 0), (0, 0)))
    eye = jnp.eye(n_maps, dtype=BF16)
    q_bd = (qp.transpose(0, 2, 1, 3)[:, :, :, None, :] * eye[None, :, None, :, None]).reshape(b, n_maps * lp, w_qk)
    kn = jnp.pad(k_new, ((0, 0), (0, page - l_new), (0, 0)))
    vn = jnp.pad(v_new, ((0, 0), (0, page - l_new), (0, 0)))
    n_pg = _tile(n_pages, PAGES_PER_STEP, 1)
    n_steps = n_pages // n_pg

    assert dk == LANES and dv % LANES == 0
    n_vc = dv // LANES

    def page_map(r, cb=0):
        return lambda bi, s, pt: (pt[bi * n_pages + jnp.minimum(s, n_steps - 1) * n_pg + r], 0, cb)

    rowspec = pl.BlockSpec((1, dk), lambda bi, s, pt: (0, 0))
    seqspec = lambda rows, w: pl.BlockSpec((None, rows, w), lambda bi, s, pt: (bi, 0, 0))
    grid_spec = pltpu.PrefetchScalarGridSpec(
        num_scalar_prefetch=1,
        grid=(b, n_steps + 1),
        in_specs=(
            [seqspec(n_maps * lp, w_qk)]
            + [pl.BlockSpec((None, page * n_maps, dk), page_map(r)) for r in range(n_pg)]
            + [pl.BlockSpec((None, page * n_heads, LANES), page_map(r, cb))
               for r in range(n_pg) for cb in range(n_vc)]
            + [seqspec(page, w_qk), seqspec(page, w_v), rowspec, rowspec, rowspec, rowspec,
               pl.BlockSpec((1, dv), lambda bi, s, pt: (0, 0))]
        ),
        out_specs=seqspec(lp, w_v),
        scratch_shapes=[
            pltpu.VMEM((n_maps * lp, 1), F32),
            pltpu.VMEM((n_maps * lp, 1), F32),
            pltpu.VMEM((n_maps * lp, w_v), F32),
        ],
    )
    out = pl.pallas_call(
        functools.partial(_attn_sample_body, n_pg=n_pg, n_steps=n_steps, n_heads=n_heads, dv=dv, lp=lp,
                          l_new=l_new),
        grid_spec=grid_spec,
        out_shape=jax.ShapeDtypeStruct((b, lp, w_v), F32),
        compiler_params=_params(("parallel", "arbitrary"), VMEM_LIMIT_BYTES),
        name="diff_attn_sample",
    )(page_table.reshape(-1), q_bd, *([cache_k] * n_pg), *([cache_v] * (n_pg * n_vc)), kn, vn, *lam_rows, subln_w)
    return out[:, :l_new]


def _gla_constants(c):
    levels = c.bit_length() - 1
    assert 1 << levels == c
    idx = np.arange(c)
    i, t = idx[:, None], idx[None, :]
    sums = [t <= i, t > i]
    masks = [i == t]
    for lvl in range(1, levels + 1):
        blk, half = 1 << lvl, 1 << (lvl - 1)
        pos = idx % blk
        mid = (idx // blk) * blk + half - 1
        right, left = pos >= half, pos < half
        sums.append(right[:, None] & (t > mid[:, None]) & (t <= i))
        sums.append(left[:, None] & (t > i) & (t <= mid[:, None]))
        masks.append((idx[:, None] // blk == idx[None, :] // blk) & right[:, None] & left[None, :])
    return (jnp.asarray(np.concatenate(sums, 0).astype(np.float32), BF16),
            jnp.asarray(np.stack(masks, 0).astype(np.float32), F32))


def _gla_body(q_ref, k_ref, v_ref, r_ref, g_ref, s0_ref, sum_ref, mask_ref, gw_ref, o_ref, sout_ref, s_sc,
              *, c, levels, dk):
    ci = pl.program_id(2)

    @pl.when(ci == 0)
    def _():
        s_sc[...] = s0_ref[...].astype(F32)

    g = g_ref[...]
    g1 = g.astype(BF16)
    rem = g - g1.astype(F32)
    g2 = rem.astype(BF16)
    g3 = (rem - g2.astype(F32)).astype(BF16)
    sums = sum_ref[...]
    seg = (jnp.dot(sums, g1, preferred_element_type=F32) + jnp.dot(sums, g2, preferred_element_type=F32)
           + jnp.dot(sums, g3, preferred_element_type=F32))
    ones = jnp.ones((c, LANES), BF16)
    b_last = (_tn_dot(g1, ones) + _tn_dot(g2, ones) + _tn_dot(g3, ones))[:, 0:1]

    q = q_ref[...] * (dk ** -0.5)
    k = k_ref[...]
    v = v_ref[...].astype(BF16)
    state = s_sc[...]
    o = jnp.dot((q * jnp.exp(seg[0:c])).astype(BF16), state.astype(BF16), preferred_element_type=F32)
    a = mask_ref[0] * _nt_dot(q.astype(BF16), k.astype(BF16))
    for lvl in range(1, levels + 1):
        qd = (q * jnp.exp(seg[2 * lvl * c:(2 * lvl + 1) * c])).astype(BF16)
        kd = (k * jnp.exp(seg[(2 * lvl + 1) * c:(2 * lvl + 2) * c])).astype(BF16)
        a = a + mask_ref[lvl] * _nt_dot(qd, kd)
    o = o + jnp.dot(a.astype(BF16), v, preferred_element_type=F32)
    k_tail = (k * jnp.exp(seg[c:2 * c])).astype(BF16)
    new_state = state * jnp.exp(b_last) + _tn_dot(k_tail, v)
    s_sc[...] = new_state

    y = o * lax.rsqrt(jnp.mean(o * o, axis=-1, keepdims=True) + NORM_EPS) * gw_ref[...]
    r = r_ref[...]
    o_ref[...] = (y * (r * _sigmoid(r))).astype(o_ref.dtype)

    @pl.when(ci == pl.num_programs(2) - 1)
    def _():
        sout_ref[...] = new_state


def _gla(q, k, v, r, g, offs, s0, gla_norm_w, c):
    b, n_heads, dk, dv = s0.shape
    rows = g.shape[1]
    n_chunks = rows // c
    levels = c.bit_length() - 1
    sums, masks = _gla_constants(c)
    assert all(off % w == 0 for off, w in zip(offs, (dk, dk, dv, dv)))
    col = lambda w, off: pl.BlockSpec(
        (None, c, w), functools.partial(lambda bi, h, ci, o: (bi, ci, h + o), o=off // w))
    const3 = lambda shp: pl.BlockSpec(shp, lambda bi, h, ci: (0,) * len(shp))
    sspec = pl.BlockSpec((None, None, dk, dv), lambda bi, h, ci: (bi, h, 0, 0))
    return pl.pallas_call(
        functools.partial(_gla_body, c=c, levels=levels, dk=dk),
        grid=(b, n_heads, n_chunks),
        in_specs=[col(dk, offs[0]), col(dk, offs[1]), col(dv, offs[2]), col(dv, offs[3]), col(dk, 0), sspec,
                  const3(sums.shape), const3(masks.shape), const3((1, dv))],
        out_specs=[col(dv, 0), sspec],
        out_shape=[jax.ShapeDtypeStruct((b, rows, n_heads * dv), BF16),
                   jax.ShapeDtypeStruct((b, n_heads, dk, dv), F32)],
        scratch_shapes=[pltpu.VMEM((dk, dv), F32)],
        compiler_params=_params(("parallel", "parallel", "arbitrary"), VMEM_LIMIT_BYTES),
        name="gla",
    )(q, k, v, r, g, s0, sums, masks, gla_norm_w.reshape(1, dv).astype(F32))


def _router_body(x_ref, w_ref, wr_ref, br_ref, h_ref, idx_ref, gate_ref):
    x = x_ref[...]
    h = (x * lax.rsqrt(jnp.mean(x * x, axis=-1, keepdims=True) + NORM_EPS)) * w_ref[...]
    h_ref[...] = h
    logits = jnp.dot(h, wr_ref[...], precision=lax.Precision.HIGHEST, preferred_element_type=F32) + br_ref[...]
    lane = lax.broadcasted_iota(jnp.int32, logits.shape, 1)
    idx_out = jnp.zeros(logits.shape, jnp.int32)
    val_out = jnp.zeros(logits.shape, F32)
    top0 = None
    for kk in range(TOP_K):
        mx = jnp.max(logits, axis=-1, keepdims=True)
        idx = jnp.min(jnp.where(logits == mx, lane, LANES), axis=-1, keepdims=True)
        top0 = mx if top0 is None else top0
        idx_out = jnp.where(lane == kk, idx, idx_out)
        val_out = jnp.where(lane == kk, jnp.exp(mx - top0), val_out)
        logits = jnp.where(lane == idx, -jnp.inf, logits)
    idx_ref[...] = idx_out
    gate_ref[...] = val_out / jnp.sum(val_out, axis=-1, keepdims=True)


def _router(x, norm_w, w_router, b_router):
    n, d = x.shape
    e = w_router.shape[1]
    assert TOP_K <= e <= LANES
    wr = jnp.zeros((d, LANES), F32).at[:, :e].set(w_router.astype(F32))
    br = jnp.full((1, LANES), -jnp.inf, F32).at[0, :e].set(b_router.astype(F32))
    tr = _tile(n, ROW_TILE, SUBLANES)
    row = lambda w: pl.BlockSpec((tr, w), lambda i: (i, 0))
    fixed = lambda r, w: pl.BlockSpec((r, w), lambda i: (0, 0))
    return pl.pallas_call(
        _router_body,
        grid=(n // tr,),
        in_specs=[row(d), fixed(1, d), fixed(d, LANES), fixed(1, LANES)],
        out_specs=[row(d), row(LANES), row(LANES)],
        out_shape=[jax.ShapeDtypeStruct((n, d), F32), jax.ShapeDtypeStruct((n, LANES), jnp.int32),
                   jax.ShapeDtypeStruct((n, LANES), F32)],
        compiler_params=_params(("parallel",), VMEM_LIMIT_BYTES),
        name="ffn_norm_router",
    )(x, norm_w.reshape(1, d).astype(F32), wr, br)


def _row_gather(src_hbm, idx_ref, buf, sem, slot, n):
    def copy(src_row, r):
        return pltpu.make_async_copy(src_hbm.at[pl.ds(src_row, 1), :], buf.at[slot, pl.ds(r, 1), :], sem.at[slot])

    def wait_one(r, carry):
        copy(0, r).wait()
        return carry

    def issue():
        for r in range(n):
            copy(idx_ref[0, r], r).start(priority=r % 2)

    def wait():
        lax.fori_loop(0, n, wait_one, 0, unroll=8)

    return issue, wait


def _gather_body(valid_ref, tok_ref, tok_next_ref, h_hbm, o_ref, buf, sem, *, rows):
    t = pl.program_id(0)
    slot = t % 2
    issue_cur, wait_cur = _row_gather(h_hbm, tok_ref, buf, sem, slot, rows)
    issue_next, _ = _row_gather(h_hbm, tok_next_ref, buf, sem, 1 - slot, rows)

    @pl.when(jnp.logical_and(t == 0, valid_ref[0] > 0))
    def _():
        issue_cur()

    last = pl.num_programs(0) - 1

    @pl.when(jnp.logical_and(t < last, valid_ref[jnp.minimum(t + 1, last)] > 0))
    def _():
        issue_next()

    @pl.when(valid_ref[t] > 0)
    def _():
        wait_cur()
        o_ref[...] = buf[slot].astype(o_ref.dtype)

    @pl.when(valid_ref[t] == 0)
    def _():
        o_ref[...] = jnp.zeros(o_ref.shape, o_ref.dtype)


def _gather_rows(h, slot_tok, tile_valid, rows):
    n_slots = slot_tok.shape[0]
    d = h.shape[1]
    n_tiles = n_slots // rows
    grid_spec = pltpu.PrefetchScalarGridSpec(
        num_scalar_prefetch=1,
        grid=(n_tiles,),
        in_specs=[
            pl.BlockSpec((None, 1, rows), lambda t, valid: (t, 0, 0), memory_space=pltpu.SMEM),
            pl.BlockSpec((None, 1, rows), lambda t, valid: (jnp.minimum(t + 1, n_tiles - 1), 0, 0),
                         memory_space=pltpu.SMEM),
            pl.BlockSpec(memory_space=pl.ANY),
        ],
        out_specs=pl.BlockSpec((rows, d), lambda t, valid: (t, 0)),
        scratch_shapes=[pltpu.VMEM((2, rows, d), F32), pltpu.SemaphoreType.DMA((2,))],
    )
    return pl.pallas_call(
        functools.partial(_gather_body, rows=rows),
        grid_spec=grid_spec,
        out_shape=jax.ShapeDtypeStruct((n_slots, d), BF16),
        compiler_params=_params(("arbitrary",), VMEM_LIMIT_BYTES),
        name="moe_gather",
    )(tile_valid, slot_tok.reshape(n_tiles, 1, rows), slot_tok.reshape(n_tiles, 1, rows), h)


def _expert_rows(n_rows, x_ref, w_refs, w_sc, o_ref, tm, finish):
    k = w_sc.shape[0]
    kc = _tile(k, MOE_CAST_CHUNK, LANES)
    acc = None
    for c in range(k // kc):
        ks = slice(c * kc, (c + 1) * kc)
        w_c = jnp.concatenate([w[ks, :].astype(BF16) for w in w_refs], axis=1)
        w_sc[ks, :] = w_c
        part = jnp.dot(x_ref[0:tm, ks], w_c, preferred_element_type=F32)
        acc = part if acc is None else acc + part
    o_ref[0:tm, :] = finish(acc).astype(o_ref.dtype)

    def tile(start, size):
        rs = pl.ds(pl.multiple_of(start, tm), size)
        o_ref[rs, :] = finish(jnp.dot(x_ref[rs, :], w_sc[...], preferred_element_type=F32)).astype(o_ref.dtype)

    n_small = (n_rows + tm - 1) // tm
    rest = n_small - 1
    n_quad = rest // 4

    def quad_tile(r, carry):
        tile(tm + r * (4 * tm), 4 * tm)
        return carry

    def zero(r, carry):
        o_ref[pl.ds(pl.multiple_of(r * tm, tm), tm), :] = jnp.zeros((tm, o_ref.shape[1]), o_ref.dtype)
        return carry

    lax.fori_loop(0, n_quad, quad_tile, 0)
    base = tm + n_quad * (4 * tm)

    @pl.when(rest % 4 >= 2)
    def _():
        tile(base, 2 * tm)

    @pl.when(rest % 2 == 1)
    def _():
        tile(base + (rest % 4 // 2) * (2 * tm), tm)

    lax.fori_loop(n_small, o_ref.shape[0] // tm, zero, 0)


def _expert_up_body(ge_ref, gr_ref, nv_ref, x_ref, wg_ref, wu_ref, bg_ref, bu_ref, o_ref, w_sc, *, tm):
    g = pl.program_id(0)
    tn = o_ref.shape[1]

    @pl.when(g < nv_ref[0])
    def _():
        bias = jnp.concatenate([bg_ref[...], bu_ref[...]], axis=1)

        def swiglu(gu):
            gu = gu + bias
            gt = jnp.minimum(gu[:, :tn], SWIGLU_LIMIT)
            up = jnp.clip(gu[:, tn:], -SWIGLU_LIMIT, SWIGLU_LIMIT)
            return (up + 1.0) * (gt * _sigmoid(SWIGLU_ALPHA * gt))

        _expert_rows(gr_ref[g], x_ref, (wg_ref, wu_ref), w_sc, o_ref, tm, swiglu)


def _expert_down_body(ge_ref, gr_ref, nv_ref, x_ref, w_ref, b_ref, o_ref, w_sc, *, tm):
    g = pl.program_id(0)

    @pl.when(g < nv_ref[0])
    def _():
        _expert_rows(gr_ref[g], x_ref, (w_ref,), w_sc, o_ref, tm, lambda y: y + b_ref[...])


def _expert_ffn(xs, group_expert, group_rows, n_valid, w_gate_up, b_gate_up, w_down, b_down, tm):
    n_groups, g_rows, d = xs.shape
    e, _, two_de = w_gate_up.shape
    de = two_de // 2
    tn_u = _tile(de, MOE_TN, LANES)
    tn_d = _tile(d, MOE_TN_DOWN, LANES)
    nj_u, nj_d = de // tn_u, d // tn_d

    def maps(nj):
        gv = lambda g, nv: jnp.minimum(g, nv[0] - 1)
        jv = lambda g, j, nv: jnp.where(g < nv[0], j, nj - 1)
        return gv, jv

    gv, jv = maps(nj_u)
    act = pl.pallas_call(
        functools.partial(_expert_up_body, tm=tm),
        grid_spec=pltpu.PrefetchScalarGridSpec(
            num_scalar_prefetch=3,
            grid=(n_groups, nj_u),
            in_specs=[
                pl.BlockSpec((None, g_rows, d), lambda g, j, ge, gr, nv: (gv(g, nv), 0, 0)),
                pl.BlockSpec((None, d, tn_u), lambda g, j, ge, gr, nv: (ge[gv(g, nv)], 0, jv(g, j, nv))),
                pl.BlockSpec((None, d, tn_u), lambda g, j, ge, gr, nv: (ge[gv(g, nv)], 0, nj_u + jv(g, j, nv))),
                pl.BlockSpec((None, 1, tn_u), lambda g, j, ge, gr, nv: (ge[gv(g, nv)], 0, jv(g, j, nv))),
                pl.BlockSpec((None, 1, tn_u), lambda g, j, ge, gr, nv: (ge[gv(g, nv)], 0, nj_u + jv(g, j, nv))),
            ],
            out_specs=pl.BlockSpec((None, g_rows, tn_u), lambda g, j, ge, gr, nv: (gv(g, nv), 0, jv(g, j, nv))),
            scratch_shapes=[pltpu.VMEM((d, 2 * tn_u), BF16)],
        ),
        out_shape=jax.ShapeDtypeStruct((n_groups, g_rows, de), BF16),
        compiler_params=_params(("arbitrary", "arbitrary"), VMEM_LIMIT_BYTES),
        name="moe_gate_up",
    )(group_expert, group_rows, n_valid, xs, w_gate_up, w_gate_up, b_gate_up.reshape(e, 1, two_de),
      b_gate_up.reshape(e, 1, two_de))

    gv, jv = maps(nj_d)
    return pl.pallas_call(
        functools.partial(_expert_down_body, tm=tm),
        grid_spec=pltpu.PrefetchScalarGridSpec(
            num_scalar_prefetch=3,
            grid=(n_groups, nj_d),
            in_specs=[
                pl.BlockSpec((None, g_rows, de), lambda g, j, ge, gr, nv: (gv(g, nv), 0, 0)),
                pl.BlockSpec((None, de, tn_d), lambda g, j, ge, gr, nv: (ge[gv(g, nv)], 0, jv(g, j, nv))),
                pl.BlockSpec((None, 1, tn_d), lambda g, j, ge, gr, nv: (ge[gv(g, nv)], 0, jv(g, j, nv))),
            ],
            out_specs=pl.BlockSpec((None, g_rows, tn_d), lambda g, j, ge, gr, nv: (gv(g, nv), 0, jv(g, j, nv))),
            scratch_shapes=[pltpu.VMEM((de, tn_d), BF16)],
        ),
        out_shape=jax.ShapeDtypeStruct((n_groups, g_rows, d), F32),
        compiler_params=_params(("arbitrary", "arbitrary"), VMEM_LIMIT_BYTES),
        name="moe_down",
    )(group_expert, group_rows, n_valid, act, w_down, b_down.reshape(e, 1, d))


def _combine_body(pos_ref, pos_next_ref, y_hbm, gate_ref, x_ref, w_ref, o_ref, buf, sem, *, rows):
    t = pl.program_id(0)
    slot = t % 2
    issue_cur, wait_cur = _row_gather(y_hbm, pos_ref, buf, sem, slot, TOP_K * rows)
    issue_next, _ = _row_gather(y_hbm, pos_next_ref, buf, sem, 1 - slot, TOP_K * rows)

    @pl.when(t == 0)
    def _():
        issue_cur()

    @pl.when(t < pl.num_programs(0) - 1)
    def _():
        issue_next()

    wait_cur()
    gates = gate_ref[...]
    x = x_ref[...]
    for kk in range(TOP_K):
        x = x + gates[:, kk:kk + 1] * buf[slot, kk * rows:(kk + 1) * rows, :]
    y = x * lax.rsqrt(jnp.mean(x * x, axis=-1, keepdims=True) + NORM_EPS)
    o_ref[...] = y * w_ref[...]


def _combine(pos_tiles, y_slots, gates, x1, norm_w, row0, n_rows, rows):
    d = x1.shape[1]
    assert row0 % rows == 0 and n_rows % rows == 0
    rb = row0 // rows
    n_tiles = n_rows // rows
    grid_spec = pltpu.PrefetchScalarGridSpec(
        num_scalar_prefetch=0,
        grid=(n_tiles,),
        in_specs=[
            pl.BlockSpec((None, 1, TOP_K * rows), lambda t: (t + rb, 0, 0), memory_space=pltpu.SMEM),
            pl.BlockSpec((None, 1, TOP_K * rows), lambda t: (jnp.minimum(t + 1, n_tiles - 1) + rb, 0, 0),
                         memory_space=pltpu.SMEM),
            pl.BlockSpec(memory_space=pl.ANY),
            pl.BlockSpec((rows, LANES), lambda t: (t + rb, 0)),
            pl.BlockSpec((rows, d), lambda t: (t + rb, 0)),
            pl.BlockSpec((1, d), lambda t: (0, 0)),
        ],
        out_specs=pl.BlockSpec((rows, d), lambda t: (t, 0)),
        scratch_shapes=[pltpu.VMEM((2, TOP_K * rows, d), F32), pltpu.SemaphoreType.DMA((2,))],
    )
    return pl.pallas_call(
        functools.partial(_combine_body, rows=rows),
        grid_spec=grid_spec,
        out_shape=jax.ShapeDtypeStruct((n_rows, d), F32),
        compiler_params=_params(("arbitrary",), VMEM_LIMIT_BYTES),
        name="moe_combine_norm",
    )(pos_tiles, pos_tiles, y_slots, gates, x1, norm_w.reshape(1, d).astype(F32))


def _routing_tables(expert_ids, n_experts, g_rows, n_groups, gather_rows):
    n = expert_ids.shape[0]
    m = n * TOP_K
    e_flat = expert_ids.reshape(-1)
    onehot = (e_flat[:, None] == jnp.arange(n_experts, dtype=jnp.int32)[None, :]).astype(jnp.int32)
    before = jnp.cumsum(onehot, axis=0) - onehot
    rank = jnp.take_along_axis(before, e_flat[:, None], axis=1)[:, 0]
    counts = jnp.sum(onehot, axis=0)
    groups_per_e = (counts + g_rows - 1) // g_rows
    g_end = jnp.cumsum(groups_per_e)
    g_base = g_end - groups_per_e
    slot = (g_base[e_flat] + rank // g_rows) * g_rows + rank % g_rows
    n_valid = g_end[-1]
    gid = jnp.arange(n_groups, dtype=jnp.int32)
    g_exp = jnp.minimum(jnp.searchsorted(g_end, gid, side="right"), n_experts - 1).astype(jnp.int32)
    g_cnt = jnp.clip(counts[g_exp] - (gid - g_base[g_exp]) * g_rows, 0, g_rows)
    g_cnt = jnp.where(gid < n_valid, g_cnt, 0).astype(jnp.int32)
    slot_tok = jnp.zeros((n_groups * g_rows,), jnp.int32).at[slot].set(jnp.arange(m, dtype=jnp.int32) // TOP_K)
    tiles_per_g = g_rows // gather_rows
    tile_start = (jnp.arange(n_groups * tiles_per_g, dtype=jnp.int32) % tiles_per_g) * gather_rows
    tile_valid = (tile_start < jnp.repeat(g_cnt, tiles_per_g)).astype(jnp.int32)
    return slot.astype(jnp.int32), slot_tok, tile_valid, g_exp, g_cnt, n_valid.astype(jnp.int32).reshape(1)


def kernel(x_prompt, x_sample, cache_k, cache_v, state_gla, page_table, norm_mix_w, w_in, lambda_q1, lambda_k1, lambda_q2, lambda_k2, subln_w, w_alpha2, b_alpha, gla_norm_w, w_branch_a, w_branch_b, w_out, norm_ffn_w, w_router, b_router, w_gate_up, b_gate_up, w_down, b_down, norm_final_w):
    bp, s_len, d = x_prompt.shape
    bs, l_new, _ = x_sample.shape
    depth, n_pool, page, n_maps, dk_a = cache_k.shape
    _, _, _, h_a, dv_a = cache_v.shape
    _, _, h_b, dk_b, dv_b = state_gla.shape
    rank = w_alpha2.shape[1]
    n_experts = w_router.shape[2]
    assert depth == 1 and bp == 1 and n_maps == 2 * h_a
    n_p, n_s = bp * s_len, bs * l_new
    n_all = n_p + n_s
    sizes = (n_maps * dk_a, n_maps * dk_a, h_a * dv_a, h_b * dk_b, h_b * dk_b, h_b * dv_b, h_b * dv_b, d, d, rank)
    offs = [int(o) for o in np.cumsum((0,) + sizes)]
    n_main = offs[9]
    assert w_in.shape[2] == offs[10]
    past = page_table.shape[1] * page

    x_all = jnp.concatenate([x_prompt.reshape(n_p, d), x_sample.reshape(n_s, d)], axis=0)
    u = _rmsnorm(x_all, norm_mix_w[0], BF16)
    z = _matmul([(u, w_in, 0)], [], n_main, _epi_plain, F32, "in_proj", MM_TM_WIDE)
    g_all = _decay(u, w_in[0, :, n_main:], w_alpha2[0], b_alpha[0])

    lam_rows = [p[0].reshape(1, dk_a).astype(F32) for p in (lambda_q1, lambda_k1, lambda_q2, lambda_k2)]
    sw = subln_w[0].reshape(1, dv_a).astype(F32)

    pos_p = jnp.arange(s_len, dtype=jnp.int32)
    q_p, kf_p, kb_p, vf_p, vb_p = _prep(z, 0, n_p, pos_p, n_maps, dk_a, h_a * dv_a, offs[0:3])
    oa_p = _attn_prompt(q_p, kb_p, vb_p, lam_rows, sw, h_a, dk_a, dv_a)
    z3 = z.reshape(1, n_all, n_main)
    c_p = _tile(s_len, GLA_CHUNK, SUBLANES)
    ob_p, st_p = _gla(z3, z3, z3, z3, g_all.reshape(1, n_all, h_b * dk_b)[:, :n_p], offs[3:7],
                      jnp.zeros((bp, h_b, dk_b, dv_b), F32), gla_norm_w[0], c_p)

    pos_s = jnp.tile(past + jnp.arange(l_new, dtype=jnp.int32), bs)
    q_s, kf_s, kb_s, vf_s, vb_s = _prep(z, n_p, n_s, pos_s, n_maps, dk_a, h_a * dv_a, offs[0:3])
    oa_s = _attn_sample(
        q_s.reshape(bs, l_new, -1), kb_s.reshape(bs, l_new, -1), vb_s.reshape(bs, l_new, -1),
        cache_k.reshape(n_pool, page * n_maps, dk_a), cache_v.reshape(n_pool, page * h_a, dv_a),
        page_table, lam_rows, sw, h_a, dk_a, dv_a)
    c_s = max(SUBLANES, 1 << (l_new - 1).bit_length())
    pad_s = lambda a: jnp.pad(a.reshape(bs, l_new, -1), ((0, 0), (0, c_s - l_new), (0, 0)))
    zs = pad_s(z[n_p:, offs[3]:offs[7]])
    o3 = offs[3]
    ob_s, st_s = _gla(zs, zs, zs, zs, pad_s(g_all[n_p:]), [o - o3 for o in offs[3:7]],
                      state_gla[0].astype(F32), gla_norm_w[0], c_s)

    oa = jnp.concatenate([oa_p, oa_s.reshape(n_s, -1).astype(BF16)], axis=0)
    ob = jnp.concatenate([ob_p.reshape(n_p, -1), ob_s[:, :l_new].reshape(n_s, -1)], axis=0)
    mix = _matmul([(oa, w_branch_a[0].astype(BF16), 0), (ob, w_branch_b[0].astype(BF16), 0)],
                  [(z, offs[7]), (z, offs[8])], d, _epi_gated_merge, BF16, "branch_merge")
    x1 = _matmul([(mix, w_out[0].astype(BF16), 0)], [(x_all, 0)], d, _epi_residual, F32, "out_proj")

    h, top_i, gates = _router(x1, norm_ffn_w[0], w_router[0], b_router[0])
    m = n_all * TOP_K
    g_rows = -(-int(math.ceil(MOE_GROUP_SLACK * m / n_experts)) // MOE_TM) * MOE_TM
    gather_rows = _tile(g_rows, GATHER_ROWS, BF16_SUBLANES)
    n_groups = -(-m // g_rows) + n_experts
    slot, slot_tok, tile_valid, g_exp, g_cnt, n_valid = _routing_tables(
        top_i[:, :TOP_K], n_experts, g_rows, n_groups, gather_rows)
    xs = _gather_rows(h, slot_tok, tile_valid, gather_rows)
    y_slots = _expert_ffn(xs.reshape(n_groups, g_rows, d), g_exp, g_cnt, n_valid,
                          w_gate_up[0], b_gate_up[0], w_down[0], b_down[0], MOE_TM)
    y_slots = y_slots.reshape(n_groups * g_rows, d)
    rows_c = _tile(math.gcd(n_p, n_s), COMBINE_ROWS, SUBLANES)
    pos_tiles = slot.reshape(n_all // rows_c, rows_c, TOP_K).transpose(0, 2, 1).reshape(n_all // rows_c, 1, TOP_K * rows_c)
    y_p = _combine(pos_tiles, y_slots, gates, x1, norm_final_w, 0, n_p, rows_c)
    y_s = _combine(pos_tiles, y_slots, gates, x1, norm_final_w, n_p, n_s, rows_c)

    return (
        y_p.reshape(bp, s_len, d),
        y_s.reshape(bs, l_new, d),
        kf_p.reshape(1, bp, s_len, n_maps, dk_a),
        vf_p.reshape(1, bp, s_len, h_a, dv_a),
        st_p.reshape(1, bp, h_b, dk_b, dv_b),
        kf_s.reshape(1, bs, l_new, n_maps, dk_a),
        vf_s.reshape(1, bs, l_new, h_a, dv_a),
        st_s.reshape(1, bs, h_b, dk_b, dv_b),
    )
```

```python
import functools
import math

import jax
import jax.numpy as jnp
import numpy as np
from jax import lax
from jax.experimental import pallas as pl
from jax.experimental.pallas import tpu as pltpu

F32 = jnp.float32
BF16 = jnp.bfloat16

ROPE_THETA = 500000.0
GLA_TAU = 16.0
TOP_K = 4
SWIGLU_LIMIT = 7.0
SWIGLU_ALPHA = 1.702
NORM_EPS = 1e-5
LAMBDA_INIT = 0.8 - 0.6 * math.exp(-0.3 * 0)

LANES = 128
SUBLANES = 8
BF16_SUBLANES = 16
VMEM_LIMIT_BYTES = 56 * 1024 * 1024

ROW_TILE = 256
MM_TM = 1024
MM_TM_WIDE = 2048
MM_TN = 512
ATTN_TILE = 1024
ATTN_ROW_BLOCK = 128
PAGES_PER_STEP = 8
GLA_CHUNK = 128
MOE_TM = 128
MOE_TN = 256
MOE_TN_DOWN = 512
MOE_CAST_CHUNK = 1024
MOE_GROUP_SLACK = 1.45
GATHER_ROWS = 128
COMBINE_ROWS = 64


def _tile(n, pref, mult):
    best = None
    for d in range(mult, min(n, pref) + 1, mult):
        if n % d == 0:
            best = d
    return best if best is not None else n


def _params(sem, vmem=None):
    return pltpu.CompilerParams(dimension_semantics=sem, vmem_limit_bytes=vmem)


def _sigmoid(x):
    return 1.0 / (1.0 + jnp.exp(-x))


def _nt_dot(a, b):
    return lax.dot_general(a, b, (((1,), (1,)), ((), ())), preferred_element_type=F32)


def _tn_dot(a, b):
    return lax.dot_general(a, b, (((0,), (0,)), ((), ())), preferred_element_type=F32)


def _rmsnorm_body(x_ref, w_ref, o_ref):
    x = x_ref[...].astype(F32)
    y = x * lax.rsqrt(jnp.mean(x * x, axis=-1, keepdims=True) + NORM_EPS)
    o_ref[...] = (y * w_ref[...]).astype(o_ref.dtype)


def _rmsnorm(x, w, out_dtype):
    n, d = x.shape
    tr = _tile(n, ROW_TILE, BF16_SUBLANES)
    return pl.pallas_call(
        _rmsnorm_body,
        grid=(n // tr,),
        in_specs=[pl.BlockSpec((tr, d), lambda i: (i, 0)), pl.BlockSpec((1, d), lambda i: (0, 0))],
        out_specs=pl.BlockSpec((tr, d), lambda i: (i, 0)),
        out_shape=jax.ShapeDtypeStruct((n, d), out_dtype),
        compiler_params=_params(("parallel",)),
        name="rmsnorm",
    )(x, w.reshape(1, d).astype(F32))


def _mm_body(*refs, n_pairs, epilogue):
    o_ref = refs[-1]
    accs = [
        jnp.dot(refs[2 * p][...].astype(BF16), refs[2 * p + 1][...].astype(BF16), preferred_element_type=F32)
        for p in range(n_pairs)
    ]
    extras = [r[...] for r in refs[2 * n_pairs:-1]]
    o_ref[...] = epilogue(accs, extras).astype(o_ref.dtype)


def _matmul(pairs, extras, n_out, epilogue, out_dtype, name, tm_pref=MM_TM):
    m = pairs[0][0].shape[0]
    tm = _tile(m, tm_pref, BF16_SUBLANES)
    tn = _tile(n_out, MM_TN, LANES)
    in_specs, args = [], []
    for a, b, off in pairs:
        k = a.shape[1]
        assert off % tn == 0 and b.shape[-2] == k
        in_specs.append(pl.BlockSpec((tm, k), lambda i, j: (i, 0)))
        if b.ndim == 3:
            assert b.shape[0] == 1
            in_specs.append(pl.BlockSpec((None, k, tn), functools.partial(lambda i, j, o: (0, 0, j + o), o=off // tn)))
        else:
            in_specs.append(pl.BlockSpec((k, tn), functools.partial(lambda i, j, o: (0, j + o), o=off // tn)))
        args += [a, b]
    for e, off in extras:
        assert off % tn == 0
        in_specs.append(pl.BlockSpec((tm, tn), functools.partial(lambda i, j, o: (i, j + o), o=off // tn)))
        args.append(e)
    return pl.pallas_call(
        functools.partial(_mm_body, n_pairs=len(pairs), epilogue=epilogue),
        grid=(m // tm, n_out // tn),
        in_specs=in_specs,
        out_specs=pl.BlockSpec((tm, tn), lambda i, j: (i, j)),
        out_shape=jax.ShapeDtypeStruct((m, n_out), out_dtype),
        compiler_params=_params(("parallel", "arbitrary"), VMEM_LIMIT_BYTES),
        name=name,
    )(*args)


def _epi_plain(accs, extras):
    return accs[0]


def _epi_gated_merge(accs, extras):
    return _sigmoid(extras[0]) * accs[0] + _sigmoid(extras[1]) * accs[1]


def _epi_residual(accs, extras):
    return extras[0] + accs[0]


def _mm_nt_body(a_ref, bt_ref, o_ref):
    o_ref[...] = _nt_dot(a_ref[...], bt_ref[...].astype(BF16)).astype(o_ref.dtype)


def _matmul_nt(a, b_t, n_out, out_dtype, name):
    m, k = a.shape
    tm = _tile(m, MM_TM_WIDE, BF16_SUBLANES)
    tn = _tile(n_out, MM_TN, LANES)
    return pl.pallas_call(
        _mm_nt_body,
        grid=(m // tm, n_out // tn),
        in_specs=[pl.BlockSpec((tm, k), lambda i, j: (i, 0)), pl.BlockSpec((None, tn, k), lambda i, j: (0, j, 0))],
        out_specs=pl.BlockSpec((tm, tn), lambda i, j: (i, j)),
        out_shape=jax.ShapeDtypeStruct((m, n_out), out_dtype),
        compiler_params=_params(("parallel", "arbitrary"), VMEM_LIMIT_BYTES),
        name=name,
    )(a, b_t)


def _decay_body(u_ref, wlr_ref, wa2_ref, ba_ref, g_ref):
    rank, d = wlr_ref.shape
    w_lr = jnp.concatenate([wlr_ref[...].astype(BF16), jnp.zeros((LANES - rank, d), BF16)], axis=0)
    a = _nt_dot(u_ref[...], w_lr)
    x = jnp.dot(a.astype(BF16), wa2_ref[...], preferred_element_type=F32) + ba_ref[...]
    g_ref[...] = (jnp.minimum(x, 0.0) - jnp.log1p(jnp.exp(-jnp.abs(x)))) * (1.0 / GLA_TAU)


def _decay(u, w_in_t, row0, w_alpha2, b_alpha):
    n, d = u.shape
    rank, c = w_alpha2.shape
    assert rank <= LANES and rank % BF16_SUBLANES == 0 and row0 % rank == 0 and w_in_t.shape == (1, row0 + rank, d)
    wa2 = jnp.zeros((LANES, c), BF16).at[:rank].set(w_alpha2.astype(BF16))
    tr = _tile(n, 2 * ROW_TILE, BF16_SUBLANES)
    return pl.pallas_call(
        _decay_body,
        grid=(n // tr,),
        in_specs=[
            pl.BlockSpec((tr, d), lambda i: (i, 0)),
            pl.BlockSpec((None, rank, d), lambda i: (0, row0 // rank, 0)),
            pl.BlockSpec((LANES, c), lambda i: (0, 0)),
            pl.BlockSpec((1, c), lambda i: (0, 0)),
        ],
        out_specs=pl.BlockSpec((tr, c), lambda i: (i, 0)),
        out_shape=jax.ShapeDtypeStruct((n, c), F32),
        compiler_params=_params(("parallel",)),
        name="gla_decay",
    )(u, w_in_t, wa2, b_alpha.reshape(1, c).astype(F32))


def _rope_tables(pos, dk):
    rot = dk // 4
    half = rot // 2
    inv_freq = jnp.power(jnp.float32(ROPE_THETA), -jnp.arange(half, dtype=F32) * (2.0 / rot))
    ang = pos.astype(F32)[:, None] * inv_freq[None, :]
    cos, sin = jnp.cos(ang), jnp.sin(ang)
    n = pos.shape[0]
    pad = jnp.zeros((n, dk - rot), F32)
    zero = jnp.zeros((n, half), F32)
    c = jnp.concatenate([cos, cos, pad + 1.0], axis=1)
    s1 = jnp.concatenate([-sin, zero, pad], axis=1)
    s2 = jnp.concatenate([zero, sin, pad], axis=1)
    return c, s1, s2


def _prep_body(zq_ref, zk_ref, zv_ref, c_ref, s1_ref, s2_ref, q_o, kf_o, kb_o, vf_o, vb_o, *, n_maps, dk):
    half = dk // 8
    c, s1, s2 = c_ref[...], s1_ref[...], s2_ref[...]
    q_scale = (dk ** -0.5) * math.log2(math.e)

    def rope(x):
        return x * c + pltpu.roll(x, dk - half, 1) * s1 + pltpu.roll(x, half, 1) * s2

    for m in range(n_maps):
        sl = slice(m * dk, (m + 1) * dk)
        q_o[:, sl] = (rope(zq_ref[:, sl]) * q_scale).astype(q_o.dtype)
        k = rope(zk_ref[:, sl])
        kf_o[:, sl] = k
        kb_o[:, sl] = k.astype(kb_o.dtype)
    v = zv_ref[...]
    vf_o[...] = v
    vb_o[...] = v.astype(vb_o.dtype)


def _prep(z, row0, n_rows, pos, n_maps, dk, w_v, offs):
    w_qk = n_maps * dk
    tr = _tile(n_rows, ROW_TILE, BF16_SUBLANES)
    assert row0 % tr == 0 and offs[0] % w_qk == 0 and offs[1] % w_qk == 0 and offs[2] % w_v == 0
    rb = row0 // tr
    c, s1, s2 = _rope_tables(pos, dk)
    zspec = lambda w, off: pl.BlockSpec((tr, w), functools.partial(lambda i, o: (i + rb, o), o=off // w))
    tspec = pl.BlockSpec((tr, dk), lambda i: (i, 0))
    ospec = lambda w: pl.BlockSpec((tr, w), lambda i: (i, 0))
    sds = lambda w, dt: jax.ShapeDtypeStruct((n_rows, w), dt)
    return pl.pallas_call(
        functools.partial(_prep_body, n_maps=n_maps, dk=dk),
        grid=(n_rows // tr,),
        in_specs=[zspec(w_qk, offs[0]), zspec(w_qk, offs[1]), zspec(w_v, offs[2]), tspec, tspec, tspec],
        out_specs=[ospec(w_qk), ospec(w_qk), ospec(w_qk), ospec(w_v), ospec(w_v)],
        out_shape=[sds(w_qk, BF16), sds(w_qk, F32), sds(w_qk, BF16), sds(w_v, F32), sds(w_v, BF16)],
        compiler_params=_params(("parallel",), VMEM_LIMIT_BYTES),
        name="qkv_rope",
    )(z, z, z, c, s1, s2)


def _lambda_value(lq1, lk1, lq2, lk2):
    a = jnp.exp(jnp.sum(lq1[...] * lk1[...], axis=-1, keepdims=True))
    b = jnp.exp(jnp.sum(lq2[...] * lk2[...], axis=-1, keepdims=True))
    return a - b + LAMBDA_INIT


def _diff_finish(a1, l1, a2, l2, lam, sw):
    o = a1 / l1 - lam * (a2 / l2)
    y = o * lax.rsqrt(jnp.mean(o * o, axis=-1, keepdims=True) + NORM_EPS)
    return (y * sw) * (1.0 - LAMBDA_INIT)


def _online_softmax_update(s, m_prev, l_prev):
    m_new = jnp.maximum(m_prev, jnp.max(s, axis=-1, keepdims=True))
    alpha = jnp.exp2(m_prev - m_new)
    p = jnp.exp2(s - m_new)
    return m_new, alpha, p, alpha * l_prev + jnp.sum(p, axis=-1, keepdims=True)


def _attn_prompt_body(qi_ref, kj_ref, q_ref, k_ref, v_ref, lq1, lk1, lq2, lk2, sw_ref, o_ref,
                      m_sc, l_sc, acc_sc, *, dk):
    t = pl.program_id(1)
    i = qi_ref[t]
    j = kj_ref[t]

    @pl.when(j == 0)
    def _():
        m_sc[...] = jnp.full(m_sc.shape, -jnp.inf, F32)
        l_sc[...] = jnp.zeros(l_sc.shape, F32)
        acc_sc[...] = jnp.zeros(acc_sc.shape, F32)

    t_blk, dv = acc_sc.shape[1], acc_sc.shape[2]
    rb = min(t_blk, ATTN_ROW_BLOCK)

    def lanes(x, width):
        return x if width == LANES else jnp.concatenate([x] * (width // LANES), axis=1)

    def step(masked):
        for c in range(2):
            for r in range(t_blk // rb):
                rows = slice(r * rb, (r + 1) * rb)
                n_k = min(t_blk, -(-(r + 1) * rb // LANES) * LANES) if masked else t_blk
                s = _nt_dot(q_ref[rows, c * dk:(c + 1) * dk], k_ref[0:n_k, c * dk:(c + 1) * dk])
                if masked:
                    row = lax.broadcasted_iota(jnp.int32, s.shape, 0) + r * rb
                    col = lax.broadcasted_iota(jnp.int32, s.shape, 1)
                    s = jnp.where(col <= row, s, -jnp.inf)
                m_prev = m_sc[c, rows, :]
                m_new = jnp.maximum(m_prev, jnp.max(s, axis=-1, keepdims=True))
                alpha = jnp.exp2(m_prev - m_new)
                p = jnp.exp2(s - lanes(m_new, n_k))
                l_sc[c, rows, :] = alpha * l_sc[c, rows, :] + jnp.sum(p, axis=-1, keepdims=True)
                acc_sc[c, rows, :] = lanes(alpha, dv) * acc_sc[c, rows, :] + jnp.dot(
                    p.astype(BF16), v_ref[0:n_k, :], preferred_element_type=F32)
                m_sc[c, rows, :] = m_new

    @pl.when(j < i)
    def _():
        step(False)

    @pl.when(j == i)
    def _():
        step(True)
        lam = _lambda_value(lq1, lk1, lq2, lk2)
        o_ref[...] = _diff_finish(acc_sc[0], l_sc[0][:, 0:1], acc_sc[1], l_sc[1][:, 0:1], lam,
                                  sw_ref[...]).astype(o_ref.dtype)


def _attn_prompt(q, k, v, lam_rows, subln_w, n_heads, dk, dv):
    s_len = q.shape[0]
    t_blk = _tile(s_len, ATTN_TILE, LANES)
    nb = s_len // t_blk
    qi = np.concatenate([np.full(i + 1, i, np.int32) for i in range(nb)])
    kj = np.concatenate([np.arange(i + 1, dtype=np.int32) for i in range(nb)])
    rowspec = pl.BlockSpec((1, dk), lambda h, t, qi, kj: (0, 0))
    grid_spec = pltpu.PrefetchScalarGridSpec(
        num_scalar_prefetch=2,
        grid=(n_heads, len(qi)),
        in_specs=[
            pl.BlockSpec((t_blk, 2 * dk), lambda h, t, qi, kj: (qi[t], h)),
            pl.BlockSpec((t_blk, 2 * dk), lambda h, t, qi, kj: (kj[t], h)),
            pl.BlockSpec((t_blk, dv), lambda h, t, qi, kj: (kj[t], h)),
            rowspec, rowspec, rowspec, rowspec,
            pl.BlockSpec((1, dv), lambda h, t, qi, kj: (0, 0)),
        ],
        out_specs=pl.BlockSpec((t_blk, dv), lambda h, t, qi, kj: (qi[t], h)),
        scratch_shapes=[
            pltpu.VMEM((2, t_blk, LANES), F32),
            pltpu.VMEM((2, t_blk, LANES), F32),
            pltpu.VMEM((2, t_blk, dv), F32),
        ],
    )
    return pl.pallas_call(
        functools.partial(_attn_prompt_body, dk=dk),
        grid_spec=grid_spec,
        out_shape=jax.ShapeDtypeStruct((s_len, n_heads * dv), BF16),
        compiler_params=_params(("parallel", "arbitrary"), VMEM_LIMIT_BYTES),
        name="diff_attn_prompt",
    )(jnp.asarray(qi), jnp.asarray(kj), q, k, v, *lam_rows, subln_w)


def _attn_sample_body(pt_ref, q_ref, *refs, n_pg, n_steps, n_heads, dv, lp, l_new):
    n_vc = dv // LANES
    k_refs, v_refs = refs[:n_pg], refs[n_pg:n_pg * (1 + n_vc)]
    kn_ref, vn_ref, lq1, lk1, lq2, lk2, sw_ref, o_ref, m_sc, l_sc, acc_sc = refs[n_pg * (1 + n_vc):]
    step = pl.program_id(1)

    @pl.when(step == 0)
    def _():
        m_sc[...] = jnp.full(m_sc.shape, -jnp.inf, F32)
        l_sc[...] = jnp.zeros(l_sc.shape, F32)
        acc_sc[...] = jnp.zeros(acc_sc.shape, F32)

    q = q_ref[...]

    def update(kb, vb, mask):
        s = _nt_dot(q, kb)
        if mask is not None:
            s = jnp.where(mask, s, -jnp.inf)
        m_new, alpha, p, l_next = _online_softmax_update(s, m_sc[...], l_sc[...])
        for h in range(n_heads):
            rows = slice(2 * h * lp, 2 * (h + 1) * lp)
            cols = slice(h * dv, (h + 1) * dv)
            acc_sc[rows, cols] = alpha[rows] * acc_sc[rows, cols] + jnp.dot(
                p[rows].astype(BF16), vb[:, cols], preferred_element_type=F32)
        m_sc[...] = m_new
        l_sc[...] = l_next

    @pl.when(step < n_steps)
    def _():
        def page_rows(refs_, groups):
            rows = refs_[0].shape[0] // groups
            return jnp.concatenate(
                [ref[pl.ds(gi, rows, stride=groups), :].astype(BF16) for gi in range(groups) for ref in refs_],
                axis=1)

        kb = jnp.concatenate([page_rows(k_refs[r:r + 1], 2 * n_heads) for r in range(n_pg)], axis=0)
        vb = jnp.concatenate([page_rows(v_refs[r * n_vc:(r + 1) * n_vc], n_heads) for r in range(n_pg)], axis=0)
        update(kb, vb, None)

    @pl.when(step == n_steps)
    def _():
        shape = (q.shape[0], kn_ref.shape[0])
        row_i = jnp.bitwise_and(lax.broadcasted_iota(jnp.int32, shape, 0), lp - 1)
        col = lax.broadcasted_iota(jnp.int32, shape, 1)
        update(kn_ref[...], vn_ref[...], jnp.logical_and(col < l_new, col <= row_i))
        lam = _lambda_value(lq1, lk1, lq2, lk2)
        for h in range(n_heads):
            r0 = 2 * h * lp
            cols = slice(h * dv, (h + 1) * dv)
            o_ref[:, cols] = _diff_finish(
                acc_sc[r0:r0 + lp, cols], l_sc[r0:r0 + lp], acc_sc[r0 + lp:r0 + 2 * lp, cols],
                l_sc[r0 + lp:r0 + 2 * lp], lam, sw_ref[...])


def _attn_sample(q_rot, k_new, v_new, cache_k, cache_v, page_table, lam_rows, subln_w, n_heads, dk, dv):
    b, l_new, w_qk = q_rot.shape
    n_maps = 2 * n_heads
    n_pages = page_table.shape[1]
    page = cache_v.shape[1] // n_heads
    w_v = n_heads * dv
    lp = max(SUBLANES, 1 << (l_new - 1).bit_length())
    assert l_new <= page
    qp = jnp.pad(q_rot.reshape(b, l_new, n_maps, dk), ((0, 0), (0, lp - l_new), (0, 0), (0, 0)))
    eye = jnp.eye(n_maps, dtype=BF16)
    q_bd = (qp.transpose(0, 2, 1, 3)[:, :, :, None, :] * eye[None, :, None, :, None]).reshape(b, n_maps * lp, w_qk)
    kn = jnp.pad(k_new, ((0, 0), (0, page - l_new), (0, 0)))
    vn = jnp.pad(v_new, ((0, 0), (0, page - l_new), (0, 0)))
    n_pg = _tile(n_pages, PAGES_PER_STEP, 1)
    n_steps = n_pages // n_pg

    assert dk == LANES and dv % LANES == 0
    n_vc = dv // LANES

    def page_map(r, cb=0):
        return lambda bi, s, pt: (pt[bi * n_pages + jnp.minimum(s, n_steps - 1) * n_pg + r], 0, cb)

    rowspec = pl.BlockSpec((1, dk), lambda bi, s, pt: (0, 0))
    seqspec = lambda rows, w: pl.BlockSpec((None, rows, w), lambda bi, s, pt: (bi, 0, 0))
    grid_spec = pltpu.PrefetchScalarGridSpec(
        num_scalar_prefetch=1,
        grid=(b, n_steps + 1),
        in_specs=(
            [seqspec(n_maps * lp, w_qk)]
            + [pl.BlockSpec((None, page * n_maps, dk), page_map(r)) for r in range(n_pg)]
            + [pl.BlockSpec((None, page * n_heads, LANES), page_map(r, cb))
               for r in range(n_pg) for cb in range(n_vc)]
            + [seqspec(page, w_qk), seqspec(page, w_v), rowspec, rowspec, rowspec, rowspec,
               pl.BlockSpec((1, dv), lambda bi, s, pt: (0, 0))]
        ),
        out_specs=seqspec(lp, w_v),
        scratch_shapes=[
            pltpu.VMEM((n_maps * lp, 1), F32),
            pltpu.VMEM((n_maps * lp, 1), F32),
            pltpu.VMEM((n_maps * lp, w_v), F32),
        ],
    )
    out = pl.pallas_call(
        functools.partial(_attn_sample_body, n_pg=n_pg, n_steps=n_steps, n_heads=n_heads, dv=dv, lp=lp,
                          l_new=l_new),
        grid_spec=grid_spec,
        out_shape=jax.ShapeDtypeStruct((b, lp, w_v), F32),
        compiler_params=_params(("parallel", "arbitrary"), VMEM_LIMIT_BYTES),
        name="diff_attn_sample",
    )(page_table.reshape(-1), q_bd, *([cache_k] * n_pg), *([cache_v] * (n_pg * n_vc)), kn, vn, *lam_rows, subln_w)
    return out[:, :l_new]


def _gla_constants(c):
    levels = c.bit_length() - 1
    assert 1 << levels == c
    idx = np.arange(c)
    i, t = idx[:, None], idx[None, :]
    sums = [t <= i, t > i]
    masks = [i == t]
    for lvl in range(1, levels + 1):
        blk, half = 1 << lvl, 1 << (lvl - 1)
        pos = idx % blk
        mid = (idx // blk) * blk + half - 1
        right, left = pos >= half, pos < half
        sums.append(right[:, None] & (t > mid[:, None]) & (t <= i))
        sums.append(left[:, None] & (t > i) & (t <= mid[:, None]))
        masks.append((idx[:, None] // blk == idx[None, :] // blk) & right[:, None] & left[None, :])
    return (jnp.asarray(np.concatenate(sums, 0).astype(np.float32), BF16),
            jnp.asarray(np.stack(masks, 0).astype(np.float32), F32))


def _gla_body(q_ref, k_ref, v_ref, r_ref, g_ref, s0_ref, sum_ref, mask_ref, gw_ref, o_ref, sout_ref, s_sc,
              *, c, levels, dk):
    ci = pl.program_id(2)

    @pl.when(ci == 0)
    def _():
        s_sc[...] = s0_ref[...].astype(F32)

    g = g_ref[...]
    g1 = g.astype(BF16)
    rem = g - g1.astype(F32)
    g2 = rem.astype(BF16)
    g3 = (rem - g2.astype(F32)).astype(BF16)
    sums = sum_ref[...]
    seg = (jnp.dot(sums, g1, preferred_element_type=F32) + jnp.dot(sums, g2, preferred_element_type=F32)
           + jnp.dot(sums, g3, preferred_element_type=F32))
    ones = jnp.ones((c, LANES), BF16)
    b_last = (_tn_dot(g1, ones) + _tn_dot(g2, ones) + _tn_dot(g3, ones))[:, 0:1]

    q = q_ref[...] * (dk ** -0.5)
    k = k_ref[...]
    v = v_ref[...].astype(BF16)
    state = s_sc[...]
    o = jnp.dot((q * jnp.exp(seg[0:c])).astype(BF16), state.astype(BF16), preferred_element_type=F32)
    a = mask_ref[0] * _nt_dot(q.astype(BF16), k.astype(BF16))
    for lvl in range(1, levels + 1):
        qd = (q * jnp.exp(seg[2 * lvl * c:(2 * lvl + 1) * c])).astype(BF16)
        kd = (k * jnp.exp(seg[(2 * lvl + 1) * c:(2 * lvl + 2) * c])).astype(BF16)
        a = a + mask_ref[lvl] * _nt_dot(qd, kd)
    o = o + jnp.dot(a.astype(BF16), v, preferred_element_type=F32)
    k_tail = (k * jnp.exp(seg[c:2 * c])).astype(BF16)
    new_state = state * jnp.exp(b_last) + _tn_dot(k_tail, v)
    s_sc[...] = new_state

    y = o * lax.rsqrt(jnp.mean(o * o, axis=-1, keepdims=True) + NORM_EPS) * gw_ref[...]
    r = r_ref[...]
    o_ref[...] = (y * (r * _sigmoid(r))).astype(o_ref.dtype)

    @pl.when(ci == pl.num_programs(2) - 1)
    def _():
        sout_ref[...] = new_state


def _gla(q, k, v, r, g, offs, s0, gla_norm_w, c):
    b, n_heads, dk, dv = s0.shape
    rows = g.shape[1]
    n_chunks = rows // c
    levels = c.bit_length() - 1
    sums, masks = _gla_constants(c)
    assert all(off % w == 0 for off, w in zip(offs, (dk, dk, dv, dv)))
    col = lambda w, off: pl.BlockSpec(
        (None, c, w), functools.partial(lambda bi, h, ci, o: (bi, ci, h + o), o=off // w))
    const3 = lambda shp: pl.BlockSpec(shp, lambda bi, h, ci: (0,) * len(shp))
    sspec = pl.BlockSpec((None, None, dk, dv), lambda bi, h, ci: (bi, h, 0, 0))
    return pl.pallas_call(
        functools.partial(_gla_body, c=c, levels=levels, dk=dk),
        grid=(b, n_heads, n_chunks),
        in_specs=[col(dk, offs[0]), col(dk, offs[1]), col(dv, offs[2]), col(dv, offs[3]), col(dk, 0), sspec,
                  const3(sums.shape), const3(masks.shape), const3((1, dv))],
        out_specs=[col(dv, 0), sspec],
        out_shape=[jax.ShapeDtypeStruct((b, rows, n_heads * dv), BF16),
                   jax.ShapeDtypeStruct((b, n_heads, dk, dv), F32)],
        scratch_shapes=[pltpu.VMEM((dk, dv), F32)],
        compiler_params=_params(("parallel", "parallel", "arbitrary"), VMEM_LIMIT_BYTES),
        name="gla",
    )(q, k, v, r, g, s0, sums, masks, gla_norm_w.reshape(1, dv).astype(F32))


def _router_body(x_ref, w_ref, wr_ref, br_ref, h_ref, idx_ref, gate_ref):
    x = x_ref[...]
    h = (x * lax.rsqrt(jnp.mean(x * x, axis=-1, keepdims=True) + NORM_EPS)) * w_ref[...]
    h_ref[...] = h
    logits = jnp.dot(h, wr_ref[...], precision=lax.Precision.HIGHEST, preferred_element_type=F32) + br_ref[...]
    lane = lax.broadcasted_iota(jnp.int32, logits.shape, 1)
    idx_out = jnp.zeros(logits.shape, jnp.int32)
    val_out = jnp.zeros(logits.shape, F32)
    top0 = None
    for kk in range(TOP_K):
        mx = jnp.max(logits, axis=-1, keepdims=True)
        idx = jnp.min(jnp.where(logits == mx, lane, LANES), axis=-1, keepdims=True)
        top0 = mx if top0 is None else top0
        idx_out = jnp.where(lane == kk, idx, idx_out)
        val_out = jnp.where(lane == kk, jnp.exp(mx - top0), val_out)
        logits = jnp.where(lane == idx, -jnp.inf, logits)
    idx_ref[...] = idx_out
    gate_ref[...] = val_out / jnp.sum(val_out, axis=-1, keepdims=True)


def _router(x, norm_w, w_router, b_router):
    n, d = x.shape
    e = w_router.shape[1]
    assert TOP_K <= e <= LANES
    wr = jnp.zeros((d, LANES), F32).at[:, :e].set(w_router.astype(F32))
    br = jnp.full((1, LANES), -jnp.inf, F32).at[0, :e].set(b_router.astype(F32))
    tr = _tile(n, ROW_TILE, SUBLANES)
    row = lambda w: pl.BlockSpec((tr, w), lambda i: (i, 0))
    fixed = lambda r, w: pl.BlockSpec((r, w), lambda i: (0, 0))
    return pl.pallas_call(
        _router_body,
        grid=(n // tr,),
        in_specs=[row(d), fixed(1, d), fixed(d, LANES), fixed(1, LANES)],
        out_specs=[row(d), row(LANES), row(LANES)],
        out_shape=[jax.ShapeDtypeStruct((n, d), F32), jax.ShapeDtypeStruct((n, LANES), jnp.int32),
                   jax.ShapeDtypeStruct((n, LANES), F32)],
        compiler_params=_params(("parallel",), VMEM_LIMIT_BYTES),
        name="ffn_norm_router",
    )(x, norm_w.reshape(1, d).astype(F32), wr, br)


def _row_gather(src_hbm, idx_ref, buf, sem, slot, n):
    def copy(src_row, r):
        return pltpu.make_async_copy(src_hbm.at[pl.ds(src_row, 1), :], buf.at[slot, pl.ds(r, 1), :], sem.at[slot])

    def wait_one(r, carry):
        copy(0, r).wait()
        return carry

    def issue():
        for r in range(n):
            copy(idx_ref[0, r], r).start(priority=r % 2)

    def wait():
        lax.fori_loop(0, n, wait_one, 0, unroll=8)

    return issue, wait


def _gather_body(valid_ref, tok_ref, tok_next_ref, h_hbm, o_ref, buf, sem, *, rows):
    t = pl.program_id(0)
    slot = t % 2
    issue_cur, wait_cur = _row_gather(h_hbm, tok_ref, buf, sem, slot, rows)
    issue_next, _ = _row_gather(h_hbm, tok_next_ref, buf, sem, 1 - slot, rows)

    @pl.when(jnp.logical_and(t == 0, valid_ref[0] > 0))
    def _():
        issue_cur()

    last = pl.num_programs(0) - 1

    @pl.when(jnp.logical_and(t < last, valid_ref[jnp.minimum(t + 1, last)] > 0))
    def _():
        issue_next()

    @pl.when(valid_ref[t] > 0)
    def _():
        wait_cur()
        o_ref[...] = buf[slot].astype(o_ref.dtype)

    @pl.when(valid_ref[t] == 0)
    def _():
        o_ref[...] = jnp.zeros(o_ref.shape, o_ref.dtype)


def _gather_rows(h, slot_tok, tile_valid, rows):
    n_slots = slot_tok.shape[0]
    d = h.shape[1]
    n_tiles = n_slots // rows
    grid_spec = pltpu.PrefetchScalarGridSpec(
        num_scalar_prefetch=1,
        grid=(n_tiles,),
        in_specs=[
            pl.BlockSpec((None, 1, rows), lambda t, valid: (t, 0, 0), memory_space=pltpu.SMEM),
            pl.BlockSpec((None, 1, rows), lambda t, valid: (jnp.minimum(t + 1, n_tiles - 1), 0, 0),
                         memory_space=pltpu.SMEM),
            pl.BlockSpec(memory_space=pl.ANY),
        ],
        out_specs=pl.BlockSpec((rows, d), lambda t, valid: (t, 0)),
        scratch_shapes=[pltpu.VMEM((2, rows, d), F32), pltpu.SemaphoreType.DMA((2,))],
    )
    return pl.pallas_call(
        functools.partial(_gather_body, rows=rows),
        grid_spec=grid_spec,
        out_shape=jax.ShapeDtypeStruct((n_slots, d), BF16),
        compiler_params=_params(("arbitrary",), VMEM_LIMIT_BYTES),
        name="moe_gather",
    )(tile_valid, slot_tok.reshape(n_tiles, 1, rows), slot_tok.reshape(n_tiles, 1, rows), h)


def _expert_rows(n_rows, x_ref, w_refs, w_sc, o_ref, tm, finish):
    k = w_sc.shape[0]
    kc = _tile(k, MOE_CAST_CHUNK, LANES)
    acc = None
    for c in range(k // kc):
        ks = slice(c * kc, (c + 1) * kc)
        w_c = jnp.concatenate([w[ks, :].astype(BF16) for w in w_refs], axis=1)
        w_sc[ks, :] = w_c
        part = jnp.dot(x_ref[0:tm, ks], w_c, preferred_element_type=F32)
        acc = part if acc is None else acc + part
    o_ref[0:tm, :] = finish(acc).astype(o_ref.dtype)

    def tile(start, size):
        rs = pl.ds(pl.multiple_of(start, tm), size)
        o_ref[rs, :] = finish(jnp.dot(x_ref[rs, :], w_sc[...], preferred_element_type=F32)).astype(o_ref.dtype)

    n_small = (n_rows + tm - 1) // tm
    rest = n_small - 1
    n_quad = rest // 4

    def quad_tile(r, carry):
        tile(tm + r * (4 * tm), 4 * tm)
        return carry

    def zero(r, carry):
        o_ref[pl.ds(pl.multiple_of(r * tm, tm), tm), :] = jnp.zeros((tm, o_ref.shape[1]), o_ref.dtype)
        return carry

    lax.fori_loop(0, n_quad, quad_tile, 0)
    base = tm + n_quad * (4 * tm)

    @pl.when(rest % 4 >= 2)
    def _():
        tile(base, 2 * tm)

    @pl.when(rest % 2 == 1)
    def _():
        tile(base + (rest % 4 // 2) * (2 * tm), tm)

    lax.fori_loop(n_small, o_ref.shape[0] // tm, zero, 0)


def _expert_up_body(ge_ref, gr_ref, nv_ref, x_ref, wg_ref, wu_ref, bg_ref, bu_ref, o_ref, w_sc, *, tm):
    g = pl.program_id(0)
    tn = o_ref.shape[1]

    @pl.when(g < nv_ref[0])
    def _():
        bias = jnp.concatenate([bg_ref[...], bu_ref[...]], axis=1)

        def swiglu(gu):
            gu = gu + bias
            gt = jnp.minimum(gu[:, :tn], SWIGLU_LIMIT)
            up = jnp.clip(gu[:, tn:], -SWIGLU_LIMIT, SWIGLU_LIMIT)
            return (up + 1.0) * (gt * _sigmoid(SWIGLU_ALPHA * gt))

        _expert_rows(gr_ref[g], x_ref, (wg_ref, wu_ref), w_sc, o_ref, tm, swiglu)


def _expert_down_body(ge_ref, gr_ref, nv_ref, x_ref, w_ref, b_ref, o_ref, w_sc, *, tm):
    g = pl.program_id(0)

    @pl.when(g < nv_ref[0])
    def _():
        _expert_rows(gr_ref[g], x_ref, (w_ref,), w_sc, o_ref, tm, lambda y: y + b_ref[...])


def _expert_ffn(xs, group_expert, group_rows, n_valid, w_gate_up, b_gate_up, w_down, b_down, tm):
    n_groups, g_rows, d = xs.shape
    e, _, two_de = w_gate_up.shape
    de = two_de // 2
    tn_u = _tile(de, MOE_TN, LANES)
    tn_d = _tile(d, MOE_TN_DOWN, LANES)
    nj_u, nj_d = de // tn_u, d // tn_d

    def maps(nj):
        gv = lambda g, nv: jnp.minimum(g, nv[0] - 1)
        jv = lambda g, j, nv: jnp.where(g < nv[0], j, nj - 1)
        return gv, jv

    gv, jv = maps(nj_u)
    act = pl.pallas_call(
        functools.partial(_expert_up_body, tm=tm),
        grid_spec=pltpu.PrefetchScalarGridSpec(
            num_scalar_prefetch=3,
            grid=(n_groups, nj_u),
            in_specs=[
                pl.BlockSpec((None, g_rows, d), lambda g, j, ge, gr, nv: (gv(g, nv), 0, 0)),
                pl.BlockSpec((None, d, tn_u), lambda g, j, ge, gr, nv: (ge[gv(g, nv)], 0, jv(g, j, nv))),
                pl.BlockSpec((None, d, tn_u), lambda g, j, ge, gr, nv: (ge[gv(g, nv)], 0, nj_u + jv(g, j, nv))),
                pl.BlockSpec((None, 1, tn_u), lambda g, j, ge, gr, nv: (ge[gv(g, nv)], 0, jv(g, j, nv))),
                pl.BlockSpec((None, 1, tn_u), lambda g, j, ge, gr, nv: (ge[gv(g, nv)], 0, nj_u + jv(g, j, nv))),
            ],
            out_specs=pl.BlockSpec((None, g_rows, tn_u), lambda g, j, ge, gr, nv: (gv(g, nv), 0, jv(g, j, nv))),
            scratch_shapes=[pltpu.VMEM((d, 2 * tn_u), BF16)],
        ),
        out_shape=jax.ShapeDtypeStruct((n_groups, g_rows, de), BF16),
        compiler_params=_params(("arbitrary", "arbitrary"), VMEM_LIMIT_BYTES),
        name="moe_gate_up",
    )(group_expert, group_rows, n_valid, xs, w_gate_up, w_gate_up, b_gate_up.reshape(e, 1, two_de),
      b_gate_up.reshape(e, 1, two_de))

    gv, jv = maps(nj_d)
    return pl.pallas_call(
        functools.partial(_expert_down_body, tm=tm),
        grid_spec=pltpu.PrefetchScalarGridSpec(
            num_scalar_prefetch=3,
            grid=(n_groups, nj_d),
            in_specs=[
                pl.BlockSpec((None, g_rows, de), lambda g, j, ge, gr, nv: (gv(g, nv), 0, 0)),
                pl.BlockSpec((None, de, tn_d), lambda g, j, ge, gr, nv: (ge[gv(g, nv)], 0, jv(g, j, nv))),
                pl.BlockSpec((None, 1, tn_d), lambda g, j, ge, gr, nv: (ge[gv(g, nv)], 0, jv(g, j, nv))),
            ],
            out_specs=pl.BlockSpec((None, g_rows, tn_d), lambda g, j, ge, gr, nv: (gv(g, nv), 0, jv(g, j, nv))),
            scratch_shapes=[pltpu.VMEM((de, tn_d), BF16)],
        ),
        out_shape=jax.ShapeDtypeStruct((n_groups, g_rows, d), F32),
        compiler_params=_params(("arbitrary", "arbitrary"), VMEM_LIMIT_BYTES),
        name="moe_down",
    )(group_expert, group_rows, n_valid, act, w_down, b_down.reshape(e, 1, d))


def _combine_body(pos_ref, pos_next_ref, y_hbm, gate_ref, x_ref, w_ref, o_ref, buf, sem, *, rows):
    t = pl.program_id(0)
    slot = t % 2
    issue_cur, wait_cur = _row_gather(y_hbm, pos_ref, buf, sem, slot, TOP_K * rows)
    issue_next, _ = _row_gather(y_hbm, pos_next_ref, buf, sem, 1 - slot, TOP_K * rows)

    @pl.when(t == 0)
    def _():
        issue_cur()

    @pl.when(t < pl.num_programs(0) - 1)
    def _():
        issue_next()

    wait_cur()
    gates = gate_ref[...]
    x = x_ref[...]
    for kk in range(TOP_K):
        x = x + gates[:, kk:kk + 1] * buf[slot, kk * rows:(kk + 1) * rows, :]
    y = x * lax.rsqrt(jnp.mean(x * x, axis=-1, keepdims=True) + NORM_EPS)
    o_ref[...] = y * w_ref[...]


def _combine(pos_tiles, y_slots, gates, x1, norm_w, row0, n_rows, rows):
    d = x1.shape[1]
    assert row0 % rows == 0 and n_rows % rows == 0
    rb = row0 // rows
    n_tiles = n_rows // rows
    grid_spec = pltpu.PrefetchScalarGridSpec(
        num_scalar_prefetch=0,
        grid=(n_tiles,),
        in_specs=[
            pl.BlockSpec((None, 1, TOP_K * rows), lambda t: (t + rb, 0, 0), memory_space=pltpu.SMEM),
            pl.BlockSpec((None, 1, TOP_K * rows), lambda t: (jnp.minimum(t + 1, n_tiles - 1) + rb, 0, 0),
                         memory_space=pltpu.SMEM),
            pl.BlockSpec(memory_space=pl.ANY),
            pl.BlockSpec((rows, LANES), lambda t: (t + rb, 0)),
            pl.BlockSpec((rows, d), lambda t: (t + rb, 0)),
            pl.BlockSpec((1, d), lambda t: (0, 0)),
        ],
        out_specs=pl.BlockSpec((rows, d), lambda t: (t, 0)),
        scratch_shapes=[pltpu.VMEM((2, TOP_K * rows, d), F32), pltpu.SemaphoreType.DMA((2,))],
    )
    return pl.pallas_call(
        functools.partial(_combine_body, rows=rows),
        grid_spec=grid_spec,
        out_shape=jax.ShapeDtypeStruct((n_rows, d), F32),
        compiler_params=_params(("arbitrary",), VMEM_LIMIT_BYTES),
        name="moe_combine_norm",
    )(pos_tiles, pos_tiles, y_slots, gates, x1, norm_w.reshape(1, d).astype(F32))


def _routing_tables(expert_ids, n_experts, g_rows, n_groups, gather_rows):
    n = expert_ids.shape[0]
    m = n * TOP_K
    e_flat = expert_ids.reshape(-1)
    onehot = (e_flat[:, None] == jnp.arange(n_experts, dtype=jnp.int32)[None, :]).astype(jnp.int32)
    before = jnp.cumsum(onehot, axis=0) - onehot
    rank = jnp.take_along_axis(before, e_flat[:, None], axis=1)[:, 0]
    counts = jnp.sum(onehot, axis=0)
    groups_per_e = (counts + g_rows - 1) // g_rows
    g_end = jnp.cumsum(groups_per_e)
    g_base = g_end - groups_per_e
    slot = (g_base[e_flat] + rank // g_rows) * g_rows + rank % g_rows
    n_valid = g_end[-1]
    gid = jnp.arange(n_groups, dtype=jnp.int32)
    g_exp = jnp.minimum(jnp.searchsorted(g_end, gid, side="right"), n_experts - 1).astype(jnp.int32)
    g_cnt = jnp.clip(counts[g_exp] - (gid - g_base[g_exp]) * g_rows, 0, g_rows)
    g_cnt = jnp.where(gid < n_valid, g_cnt, 0).astype(jnp.int32)
    slot_tok = jnp.zeros((n_groups * g_rows,), jnp.int32).at[slot].set(jnp.arange(m, dtype=jnp.int32) // TOP_K)
    tiles_per_g = g_rows // gather_rows
    tile_start = (jnp.arange(n_groups * tiles_per_g, dtype=jnp.int32) % tiles_per_g) * gather_rows
    tile_valid = (tile_start < jnp.repeat(g_cnt, tiles_per_g)).astype(jnp.int32)
    return slot.astype(jnp.int32), slot_tok, tile_valid, g_exp, g_cnt, n_valid.astype(jnp.int32).reshape(1)


def kernel(x_prompt, x_sample, cache_k, cache_v, state_gla, page_table, norm_mix_w, w_in, lambda_q1, lambda_k1, lambda_q2, lambda_k2, subln_w, w_alpha2, b_alpha, gla_norm_w, w_branch_a, w_branch_b, w_out, norm_ffn_w, w_router, b_router, w_gate_up, b_gate_up, w_down, b_down, norm_final_w):
    bp, s_len, d = x_prompt.shape
    bs, l_new, _ = x_sample.shape
    depth, n_pool, page, n_maps, dk_a = cache_k.shape
    _, _, _, h_a, dv_a = cache_v.shape
    _, _, h_b, dk_b, dv_b = state_gla.shape
    rank = w_alpha2.shape[1]
    n_experts = w_router.shape[2]
    assert depth == 1 and bp == 1 and n_maps == 2 * h_a
    n_p, n_s = bp * s_len, bs * l_new
    n_all = n_p + n_s
    sizes = (n_maps * dk_a, n_maps * dk_a, h_a * dv_a, h_b * dk_b, h_b * dk_b, h_b * dv_b, h_b * dv_b, d, d, rank)
    offs = [int(o) for o in np.cumsum((0,) + sizes)]
    n_main = offs[9]
    assert w_in.shape[2] == offs[10]
    past = page_table.shape[1] * page

    x_all = jnp.concatenate([x_prompt.reshape(n_p, d), x_sample.reshape(n_s, d)], axis=0)
    u = _rmsnorm(x_all, norm_mix_w[0], BF16)
    w_in_t = jnp.transpose(w_in, (0, 2, 1))
    z = _matmul_nt(u, w_in_t, n_main, F32, "in_proj")
    g_all = _decay(u, w_in_t, n_main, w_alpha2[0], b_alpha[0])

    lam_rows = [p[0].reshape(1, dk_a).astype(F32) for p in (lambda_q1, lambda_k1, lambda_q2, lambda_k2)]
    sw = subln_w[0].reshape(1, dv_a).astype(F32)

    pos_p = jnp.arange(s_len, dtype=jnp.int32)
    q_p, kf_p, kb_p, vf_p, vb_p = _prep(z, 0, n_p, pos_p, n_maps, dk_a, h_a * dv_a, offs[0:3])
    oa_p = _attn_prompt(q_p, kb_p, vb_p, lam_rows, sw, h_a, dk_a, dv_a)
    z3 = z.reshape(1, n_all, n_main)
    c_p = _tile(s_len, GLA_CHUNK, SUBLANES)
    ob_p, st_p = _gla(z3, z3, z3, z3, g_all.reshape(1, n_all, h_b * dk_b)[:, :n_p], offs[3:7],
                      jnp.zeros((bp, h_b, dk_b, dv_b), F32), gla_norm_w[0], c_p)

    pos_s = jnp.tile(past + jnp.arange(l_new, dtype=jnp.int32), bs)
    q_s, kf_s, kb_s, vf_s, vb_s = _prep(z, n_p, n_s, pos_s, n_maps, dk_a, h_a * dv_a, offs[0:3])
    oa_s = _attn_sample(
        q_s.reshape(bs, l_new, -1), kb_s.reshape(bs, l_new, -1), vb_s.reshape(bs, l_new, -1),
        cache_k.reshape(n_pool, page * n_maps, dk_a), cache_v.reshape(n_pool, page * h_a, dv_a),
        page_table, lam_rows, sw, h_a, dk_a, dv_a)
    c_s = max(SUBLANES, 1 << (l_new - 1).bit_length())
    pad_s = lambda a: jnp.pad(a.reshape(bs, l_new, -1), ((0, 0), (0, c_s - l_new), (0, 0)))
    zs = pad_s(z[n_p:, offs[3]:offs[7]])
    o3 = offs[3]
    ob_s, st_s = _gla(zs, zs, zs, zs, pad_s(g_all[n_p:]), [o - o3 for o in offs[3:7]],
                      state_gla[0].astype(F32), gla_norm_w[0], c_s)

    oa = jnp.concatenate([oa_p, oa_s.reshape(n_s, -1).astype(BF16)], axis=0)
    ob = jnp.concatenate([ob_p.reshape(n_p, -1), ob_s[:, :l_new].reshape(n_s, -1)], axis=0)
    mix = _matmul([(oa, w_branch_a[0].astype(BF16), 0), (ob, w_branch_b[0].astype(BF16), 0)],
                  [(z, offs[7]), (z, offs[8])], d, _epi_gated_merge, BF16, "branch_merge")
    x1 = _matmul([(mix, w_out[0].astype(BF16), 0)], [(x_all, 0)], d, _epi_residual, F32, "out_proj")

    h, top_i, gates = _router(x1, norm_ffn_w[0], w_router[0], b_router[0])
    m = n_all * TOP_K
    g_rows = -(-int(math.ceil(MOE_GROUP_SLACK * m / n_experts)) // MOE_TM) * MOE_TM
    gather_rows = _tile(g_rows, GATHER_ROWS, BF16_SUBLANES)
    n_groups = -(-m // g_rows) + n_experts
    slot, slot_tok, tile_valid, g_exp, g_cnt, n_valid = _routing_tables(
        top_i[:, :TOP_K], n_experts, g_rows, n_groups, gather_rows)
    xs = _gather_rows(h, slot_tok, tile_valid, gather_rows)
    y_slots = _expert_ffn(xs.reshape(n_groups, g_rows, d), g_exp, g_cnt, n_valid,
                          w_gate_up[0], b_gate_up[0], w_down[0], b_down[0], MOE_TM)
    y_slots = y_slots.reshape(n_groups * g_rows, d)
    rows_c = _tile(math.gcd(n_p, n_s), COMBINE_ROWS, SUBLANES)
    pos_tiles = slot.reshape(n_all // rows_c, rows_c, TOP_K).transpose(0, 2, 1).reshape(n_all // rows_c, 1, TOP_K * rows_c)
    y_p = _combine(pos_tiles, y_slots, gates, x1, norm_final_w, 0, n_p, rows_c)
    y_s = _combine(pos_tiles, y_slots, gates, x1, norm_final_w, n_p, n_s, rows_c)

    return (
        y_p.reshape(bp, s_len, d),
        y_s.reshape(bs, l_new, d),
        kf_p.reshape(1, bp, s_len, n_maps, dk_a),
        vf_p.reshape(1, bp, s_len, h_a, dv_a),
        st_p.reshape(1, bp, h_b, dk_b, dv_b),
        kf_s.reshape(1, bs, l_new, n_maps, dk_a),
        vf_s.reshape(1, bs, l_new, h_a, dv_a),
        st_s.reshape(1, bs, h_b, dk_b, dv_b),
    )
```
